```python
import math
import jax, jax.numpy as jnp
from jax import lax
import numpy as np

D_MODEL = 1024
BATCH = 8
SEQ = 4096
DEPTH = 2

CHUNK = 64
Q_BLOCK = 128
N_GROUPS = 4
GROUP_WIDTH = D_MODEL // N_GROUPS
D_MIX = N_GROUPS * GROUP_WIDTH
HEAD_DIM = 64
N_HEADS_G = GROUP_WIDTH // HEAD_DIM
DIFF_D = HEAD_DIM // 2
Q_LORA = 256
KV_LORA = 128
QK_NOPE = 64
QK_ROPE = 32
V_HEAD = HEAD_DIM
ROPE_THETA = 10000.0
IDX_HEADS = 8
IDX_DIM = 32
TOPK_MAX = 256
BAND_CHUNKS = 8
REL_CLIP = 128
D_FF = 2816
N_EXPERTS = 8
TOP_K = 2
D_FF_EXPERT = 3584
N_DENSE = (DEPTH + 1) // 2
N_MOE = DEPTH // 2
EPS = 1e-6

IN_LAYOUT = (
    ('a_q', N_HEADS_G * 2 * DIFF_D), ('a_k', N_HEADS_G * 2 * DIFF_D), ('a_v', N_HEADS_G * 2 * DIFF_D),
    ('b_qd', Q_LORA), ('b_kvd', KV_LORA), ('b_kr', QK_ROPE),
    ('c_q', GROUP_WIDTH), ('c_k', GROUP_WIDTH), ('c_v', GROUP_WIDTH),
    ('c_qi', IDX_HEADS * IDX_DIM), ('c_ki', IDX_DIM), ('c_wi', IDX_HEADS),
    ('d_q', GROUP_WIDTH), ('d_k', GROUP_WIDTH), ('d_v', GROUP_WIDTH),
)
D_IN = int(sum(w for _, w in IN_LAYOUT))
IN_SPLITS = tuple(int(v) for v in np.cumsum([w for _, w in IN_LAYOUT])[:-1])

kernel_name = 'hybrid_chunk_causal_encoder'


def rmsnorm(x, g):
    xf = x.astype(jnp.float32)
    y = xf * lax.rsqrt(jnp.mean(xf * xf, axis=-1, keepdims=True) + EPS)
    return (y * g.astype(jnp.float32)).astype(x.dtype)


def adaln(x, g, shift, scale):
    return rmsnorm(x, g) * (1 + scale[:, None, :]) + shift[:, None, :]


def alibi_slopes(n):
    return 2.0 ** (-8.0 * jnp.arange(1, n + 1, dtype=jnp.float32) / n)


def rope_tables(s):
    inv = ROPE_THETA ** (-jnp.arange(0, QK_ROPE, 2, dtype=jnp.float32) / QK_ROPE)
    ang = jnp.arange(s, dtype=jnp.float32)[:, None] * inv[None, :]
    return jnp.cos(ang), jnp.sin(ang)


def apply_rope(x, cos, sin):
    x1, x2 = jnp.split(x, 2, axis=-1)
    cs = cos[None, :, None, :].astype(x.dtype)
    sn = sin[None, :, None, :].astype(x.dtype)
    return jnp.concatenate([x1 * cs - x2 * sn, x1 * sn + x2 * cs], axis=-1)


def to_blocks(a):
    b, s = a.shape[:2]
    return jnp.moveaxis(a.reshape(b, s // Q_BLOCK, Q_BLOCK, *a.shape[2:]), 1, 0)


def from_blocks(a):
    nb, b, qb = a.shape[:3]
    return jnp.moveaxis(a, 0, 1).reshape(b, nb * qb, *a.shape[3:])


def sweep_query_blocks(fn, *qs):
    nb = qs[0].shape[1] // Q_BLOCK
    out = lax.map(lambda a: fn(*a), (jnp.arange(nb), *[to_blocks(q) for q in qs]))
    return from_blocks(out)


def chunk_mask(blk, s_len):
    t = blk * Q_BLOCK + jnp.arange(Q_BLOCK)
    s = jnp.arange(s_len)
    return t, s, (s[None, :] // CHUNK) <= (t[:, None] // CHUNK)


def diff_attention(q, k, v, lam_vecs, norm_g, layer_idx):
    b, s_len, h = q.shape[:3]
    lam_init = 0.8 - 0.6 * math.exp(-0.3 * layer_idx)
    lv = lam_vecs.astype(jnp.float32)
    lam = jnp.exp(jnp.sum(lv[0] * lv[1])) - jnp.exp(jnp.sum(lv[2] * lv[3])) + lam_init
    slopes = alibi_slopes(h)
    scale = DIFF_D ** -0.5

    def block(blk, qb):
        t, s, mask = chunk_mask(blk, s_len)
        bias = -slopes[:, None, None] * jnp.abs(t[:, None] - s[None, :]).astype(jnp.float32)
        bias = jnp.where(mask[None], bias, -jnp.inf)
        logits = jnp.einsum('bqhcd,bshcd->bchqs', qb, k).astype(jnp.float32) * scale + bias[None, None]
        p = jax.nn.softmax(logits, axis=-1)
        p = p[:, 0] - lam * p[:, 1]
        return jnp.einsum('bhqs,bshe->bqhe', p.astype(v.dtype), v)

    o = sweep_query_blocks(block, q)
    o = rmsnorm(o, norm_g) * (1.0 - lam_init)
    return o.reshape(b, s_len, h * 2 * DIFF_D)


def latent_attention(qd, kvd, kr, q_norm_g, kv_norm_g, w_uq, w_ukv, cos, sin):
    b, s_len = qd.shape[:2]
    h = N_HEADS_G
    q = (rmsnorm(qd, q_norm_g) @ w_uq).reshape(b, s_len, h, QK_NOPE + QK_ROPE)
    q = jnp.concatenate([q[..., :QK_NOPE], apply_rope(q[..., QK_NOPE:], cos, sin)], axis=-1)
    kv = (rmsnorm(kvd, kv_norm_g) @ w_ukv).reshape(b, s_len, h, QK_NOPE + V_HEAD)
    k_nope, v = kv[..., :QK_NOPE], kv[..., QK_NOPE:]
    k_rope = apply_rope(kr[:, :, None, :], cos, sin)
    k = jnp.concatenate([k_nope, jnp.broadcast_to(k_rope, (b, s_len, h, QK_ROPE))], axis=-1)
    scale = (QK_NOPE + QK_ROPE) ** -0.5

    def block(blk, qb):
        _, _, mask = chunk_mask(blk, s_len)
        logits = jnp.einsum('bqhd,bshd->bhqs', qb, k).astype(jnp.float32) * scale
        logits = jnp.where(mask[None, None], logits, -jnp.inf)
        p = jax.nn.softmax(logits, axis=-1)
        return jnp.einsum('bhqs,bshd->bqhd', p.astype(v.dtype), v)

    return sweep_query_blocks(block, q).reshape(b, s_len, h * V_HEAD)


def indexed_sparse_attention(q, k, v, q_idx, k_idx, w_idx):
    b, s_len, h, dh = q.shape
    n_sel = min(TOPK_MAX, s_len // 4)
    slopes = alibi_slopes(h)
    scale = dh ** -0.5
    w = w_idx.astype(jnp.float32) * (IDX_HEADS ** -0.5 * IDX_DIM ** -0.5)

    def block(blk, qb, qib, wb):
        t, _, mask = chunk_mask(blk, s_len)
        score = jax.nn.relu(jnp.einsum('bqhd,bsd->bqhs', qib, k_idx).astype(jnp.float32))
        score = jnp.einsum('bqhs,bqh->bqs', score, wb)
        score = jnp.where(mask[None], score, -jnp.inf)
        top_val, top_idx = lax.top_k(score, n_sel)
        valid = jnp.isfinite(top_val)
        k_sel = jax.vmap(lambda kk, ii: kk[ii])(k, top_idx)
        v_sel = jax.vmap(lambda vv, ii: vv[ii])(v, top_idx)
        logits = jnp.einsum('bqhd,bqnhd->bhqn', qb, k_sel).astype(jnp.float32) * scale
        dist = jnp.abs(t[None, :, None] - top_idx).astype(jnp.float32)
        logits = logits - slopes[None, :, None, None] * dist[:, None]
        logits = jnp.where(valid[:, None], logits, -jnp.inf)
        p = jax.nn.softmax(logits, axis=-1)
        return jnp.einsum('bhqn,bqnhd->bqhd', p.astype(v.dtype), v_sel)

    return sweep_query_blocks(block, q, q_idx, w).reshape(b, s_len, h * dh)


def chunk_band_attention(q, k, v, rel_bias):
    b, s_len, h, dh = q.shape
    nc = s_len // CHUNK
    band = (BAND_CHUNKS + 1) * CHUNK
    qc = q.reshape(b, nc, CHUNK, h, dh)
    pad = ((0, 0), (BAND_CHUNKS * CHUNK, 0), (0, 0), (0, 0))
    kp = jnp.pad(k, pad).reshape(b, nc + BAND_CHUNKS, CHUNK, h, dh)
    vp = jnp.pad(v, pad).reshape(b, nc + BAND_CHUNKS, CHUNK, h, dh)
    k_band = jnp.concatenate([kp[:, j:j + nc] for j in range(BAND_CHUNKS + 1)], axis=2)
    v_band = jnp.concatenate([vp[:, j:j + nc] for j in range(BAND_CHUNKS + 1)], axis=2)
    i = jnp.arange(CHUNK)
    j = jnp.arange(band)
    rel = BAND_CHUNKS * CHUNK + i[:, None] - j[None, :]
    bias = rel_bias[:, jnp.clip(rel, -REL_CLIP, REL_CLIP) + REL_CLIP].astype(jnp.float32)
    key_chunk = jnp.arange(nc)[:, None] - BAND_CHUNKS + (j // CHUNK)[None, :]
    valid = key_chunk >= 0
    logits = jnp.einsum('bnqhd,bnkhd->bnhqk', qc, k_band).astype(jnp.float32) * (dh ** -0.5)
    logits = logits + bias[None, None]
    logits = jnp.where(valid[None, :, None, None, :], logits, -jnp.inf)
    p = jax.nn.softmax(logits, axis=-1)
    o = jnp.einsum('bnhqk,bnkhd->bnqhd', p.astype(v.dtype), v_band)
    return o.reshape(b, s_len, h * dh)


def swiglu(h, w1, w3, w2):
    return (jax.nn.silu(h @ w1) * (h @ w3)) @ w2


def moe_swiglu(h, router, w1, w3, w2):
    b, s_len, d = h.shape
    ht = h.reshape(-1, d)
    logits = (ht @ router).astype(jnp.float32)
    top_val, top_idx = lax.top_k(logits, TOP_K)
    gates = jax.nn.softmax(top_val, axis=-1)
    combine = jnp.sum(jax.nn.one_hot(top_idx, N_EXPERTS, dtype=jnp.float32) * gates[..., None], axis=1)
    out = jnp.zeros_like(ht)
    for e in range(N_EXPERTS):
        out = out + combine[:, e:e + 1].astype(ht.dtype) * swiglu(ht, w1[e], w3[e], w2[e])
    return out.reshape(b, s_len, d)


def setup_inputs(seed: int = 0) -> dict:
    key = jax.random.key(seed)
    ks = jax.random.split(key, 24)
    h = N_HEADS_G

    def nrm(k, shape, fan_in, s=1.0):
        return jax.random.normal(k, shape, jnp.float32) * (s * fan_in ** -0.5)

    def gain(k, shape):
        return 1.0 + 0.1 * jax.random.normal(k, shape, jnp.float32)

    return {
        'x': jax.random.normal(ks[0], (BATCH, SEQ, D_MODEL), jnp.float32),
        'c': jax.random.normal(ks[1], (BATCH, D_MODEL), jnp.float32),
        'ada_w': nrm(ks[2], (DEPTH, D_MODEL, 6 * D_MODEL), D_MODEL, 0.5),
        'ada_b': 0.02 * jax.random.normal(ks[3], (DEPTH, 6 * D_MODEL), jnp.float32),
        'mix_norm_g': gain(ks[4], (DEPTH, D_MODEL)),
        'ffn_norm_g': gain(ks[5], (DEPTH, D_MODEL)),
        'w_in': nrm(ks[6], (DEPTH, D_MODEL, D_IN), D_MODEL),
        'w_out': nrm(ks[7], (DEPTH, D_MIX, D_MODEL), D_MIX),
        'diff_lambda': 0.1 * jax.random.normal(ks[8], (DEPTH, 4, DIFF_D), jnp.float32),
        'diff_norm_g': gain(ks[9], (DEPTH, 2 * DIFF_D)),
        'mla_q_norm_g': gain(ks[10], (DEPTH, Q_LORA)),
        'mla_kv_norm_g': gain(ks[11], (DEPTH, KV_LORA)),
        'mla_w_uq': nrm(ks[12], (DEPTH, Q_LORA, h * (QK_NOPE + QK_ROPE)), Q_LORA),
        'mla_w_ukv': nrm(ks[13], (DEPTH, KV_LORA, h * (QK_NOPE + V_HEAD)), KV_LORA),
        'band_rel_bias': 0.2 * jax.random.normal(ks[14], (DEPTH, h, 2 * REL_CLIP + 1), jnp.float32),
        'ffn_w1': nrm(ks[15], (N_DENSE, D_MODEL, D_FF), D_MODEL),
        'ffn_w3': nrm(ks[16], (N_DENSE, D_MODEL, D_FF), D_MODEL),
        'ffn_w2': nrm(ks[17], (N_DENSE, D_FF, D_MODEL), D_FF),
        'moe_router': nrm(ks[18], (N_MOE, D_MODEL, N_EXPERTS), D_MODEL),
        'moe_w1': nrm(ks[19], (N_MOE, N_EXPERTS, D_MODEL, D_FF_EXPERT), D_MODEL),
        'moe_w3': nrm(ks[20], (N_MOE, N_EXPERTS, D_MODEL, D_FF_EXPERT), D_MODEL),
        'moe_w2': nrm(ks[21], (N_MOE, N_EXPERTS, D_FF_EXPERT, D_MODEL), D_FF_EXPERT),
        'final_norm_g': gain(ks[22], (D_MODEL,)),
    }


def reference(x, c, ada_w, ada_b, mix_norm_g, ffn_norm_g, w_in, w_out, diff_lambda, diff_norm_g,
              mla_q_norm_g, mla_kv_norm_g, mla_w_uq, mla_w_ukv, band_rel_bias,
              ffn_w1, ffn_w3, ffn_w2, moe_router, moe_w1, moe_w3, moe_w2, final_norm_g):
    b, s_len, _ = x.shape
    h = N_HEADS_G
    cos, sin = rope_tables(s_len)
    for l in range(DEPTH):
        mod = jax.nn.silu(c) @ ada_w[l] + ada_b[l]
        sh_m, sc_m, g_m, sh_f, sc_f, g_f = jnp.split(mod, 6, axis=-1)
        hn = adaln(x, mix_norm_g[l], sh_m, sc_m)
        (a_q, a_k, a_v, b_qd, b_kvd, b_kr, c_q, c_k, c_v,
         c_qi, c_ki, c_wi, d_q, d_k, d_v) = jnp.split(hn @ w_in[l], IN_SPLITS, axis=-1)
        o_a = diff_attention(a_q.reshape(b, s_len, h, 2, DIFF_D), a_k.reshape(b, s_len, h, 2, DIFF_D),
                             a_v.reshape(b, s_len, h, 2 * DIFF_D), diff_lambda[l], diff_norm_g[l], l)
        o_b = latent_attention(b_qd, b_kvd, b_kr, mla_q_norm_g[l], mla_kv_norm_g[l],
                               mla_w_uq[l], mla_w_ukv[l], cos, sin)
        o_c = indexed_sparse_attention(c_q.reshape(b, s_len, h, HEAD_DIM), c_k.reshape(b, s_len, h, HEAD_DIM),
                                       c_v.reshape(b, s_len, h, HEAD_DIM),
                                       c_qi.reshape(b, s_len, IDX_HEADS, IDX_DIM), c_ki, c_wi)
        o_d = chunk_band_attention(d_q.reshape(b, s_len, h, HEAD_DIM), d_k.reshape(b, s_len, h, HEAD_DIM),
                                   d_v.reshape(b, s_len, h, HEAD_DIM), band_rel_bias[l])
        y = jnp.concatenate([o_a, o_b, o_c, o_d], axis=-1) @ w_out[l]
        x = x + g_m[:, None, :] * y
        hn = adaln(x, ffn_norm_g[l], sh_f, sc_f)
        if l % 2 == 0:
            f = swiglu(hn, ffn_w1[l // 2], ffn_w3[l // 2], ffn_w2[l // 2])
        else:
            f = moe_swiglu(hn, moe_router[l // 2], moe_w1[l // 2], moe_w3[l // 2], moe_w2[l // 2])
        x = x + g_f[:, None, :] * f
    return rmsnorm(x, final_norm_g)
```

```python
import functools
import math

import numpy as np
import jax
import jax.numpy as jnp
from jax import lax
from jax.experimental import pallas as pl
from jax.experimental.pallas import tpu as pltpu

F32 = jnp.float32
BF16 = jnp.bfloat16
I32 = jnp.int32

D_MODEL = 1024
CHUNK = 64
N_HEADS = 4
HEAD_DIM = 64
DIFF_D = 32
Q_LORA = 256
KV_LORA = 128
QK_NOPE = 64
QK_ROPE = 32
ROPE_THETA = 10000.0
IDX_HEADS = 8
IDX_DIM = 32
TOPK_MAX = 256
BAND_CHUNKS = 8
REL_CLIP = 128
N_EXPERTS = 8
EPS = 1e-6

LOG2E = 1.4426950408889634
NEG = -1e30
INT_MIN = -(2 ** 31)

LANES = 128
VMEM_LIMIT = 56 * 1024 * 1024
TQ = 256
TM_PROJ = 512
TM_FFN = 512
N_MAIN = 2560
N_SIDE = 512


def _cparams(sem):
    return pltpu.CompilerParams(dimension_semantics=sem, vmem_limit_bytes=VMEM_LIMIT)


def _dot(a, b):
    return jnp.dot(a, b, preferred_element_type=F32)


def _silu(x):
    return x / (1.0 + jnp.exp(-x))


def _rms(x, g):
    return x * lax.rsqrt(jnp.mean(x * x, axis=-1, keepdims=True) + EPS) * g


def _adaln(x, g, shift, scale):
    return _rms(x, g) * (1.0 + scale) + shift


def _mod_kernel(c_ref, w_ref, b_ref, o_ref):
    sc = _silu(c_ref[...]).astype(BF16)
    o_ref[...] = _dot(sc, w_ref[...].astype(BF16)) + b_ref[...]


def _modulation(c, ada_w, ada_b):
    depth, d, n6 = ada_w.shape
    b = c.shape[0]
    tn = 1536
    return pl.pallas_call(
        _mod_kernel,
        grid=(depth, n6 // tn),
        in_specs=[
            pl.BlockSpec((b, d), lambda l, j: (0, 0)),
            pl.BlockSpec((None, d, tn), lambda l, j: (l, 0, j)),
            pl.BlockSpec((None, 1, tn), lambda l, j: (l, 0, j)),
        ],
        out_specs=pl.BlockSpec((None, b, tn), lambda l, j: (l, 0, j)),
        out_shape=jax.ShapeDtypeStruct((depth, b, n6), F32),
        compiler_params=_cparams(("arbitrary", "arbitrary")),
        name="modulation",
    )(c, ada_w, ada_b.reshape(depth, 1, n6))


def _inproj_kernel(x_ref, g_ref, sh_ref, sc_ref, w_ref, cs_ref, main_ref, side_ref):
    hn = _adaln(x_ref[...], g_ref[...], sh_ref[...], sc_ref[...]).astype(BF16)
    step = 512
    for c0 in range(0, N_MAIN + N_SIDE, step):
        acc = _dot(hn, w_ref[:, c0:c0 + step]) * cs_ref[:, c0:c0 + step]
        if c0 < N_MAIN:
            main_ref[:, c0:c0 + step] = acc.astype(BF16)
        else:
            side_ref[:, c0 - N_MAIN:c0 - N_MAIN + step] = acc


def _inproj(x2, g, mod3, w, cs, s_len):
    n, d = x2.shape
    tm = TM_PROJ
    tpb = s_len // tm
    nc = N_MAIN + N_SIDE
    return pl.pallas_call(
        _inproj_kernel,
        grid=(n // tm,),
        in_specs=[
            pl.BlockSpec((tm, d), lambda i: (i, 0)),
            pl.BlockSpec((1, d), lambda i: (0, 0)),
            pl.BlockSpec((None, 1, d), lambda i: (i // tpb, 0, 0)),
            pl.BlockSpec((None, 1, d), lambda i: (i // tpb, 0, 1)),
            pl.BlockSpec((d, nc), lambda i: (0, 0)),
            pl.BlockSpec((1, nc), lambda i: (0, 0)),
        ],
        out_specs=[
            pl.BlockSpec((tm, N_MAIN), lambda i: (i, 0)),
            pl.BlockSpec((tm, N_SIDE), lambda i: (i, 0)),
        ],
        out_shape=[
            jax.ShapeDtypeStruct((n, N_MAIN), BF16),
            jax.ShapeDtypeStruct((n, N_SIDE), F32),
        ],
        compiler_params=_cparams(("arbitrary",)),
        name="inproj",
    )(x2, g, mod3, mod3, w, cs)


def _softmax_update(s, m, acc_ref, slot, v):
    m_new = jnp.maximum(m, jnp.max(s, axis=-1, keepdims=True))
    alpha = jnp.exp2(m - m_new)
    p = jnp.exp2(s - m_new)
    acc_ref[slot] = acc_ref[slot] * alpha + _dot(p.astype(BF16), v)
    return m_new


def _normalized(acc_ref, slot):
    a = acc_ref[slot]
    return a[:, :HEAD_DIM] / a[:, HEAD_DIM:HEAD_DIM + 1]


def _attn_a_kernel(q_ref, kt_ref, v_ref, offb_ref, diagb_ref, slope_ref, lamv_ref, g_ref,
                   o_ref, acc_ref, *, lam_init):
    qb = pl.program_id(1)
    tq = q_ref.shape[0]
    nhm = 2 * N_HEADS
    q = q_ref[...]
    qs = [q[:, i * DIFF_D:(i + 1) * DIFF_D] for i in range(nhm)]
    acc_ref[...] = jnp.zeros_like(acc_ref)

    def block(kb, ms, bias_ref, diag):
        k0 = pl.multiple_of(kb * tq, tq)
        out = []
        for i in range(nhm):
            h = i // 2
            kt = kt_ref[i * DIFF_D:(i + 1) * DIFF_D, pl.ds(k0, tq)]
            v = v_ref[pl.ds(k0, tq), h * LANES:(h + 1) * LANES]
            s = _dot(qs[i], kt) + bias_ref[h]
            if not diag:
                s = s - slope_ref[h] * ((qb - kb) * tq).astype(F32)
            out.append(_softmax_update(s, ms[i], acc_ref, i, v))
        return tuple(out)

    ms = tuple(jnp.full((tq, 1), NEG, F32) for _ in range(nhm))
    ms = block(qb, ms, diagb_ref, True)
    ms = lax.fori_loop(0, qb, lambda kb, c: block(kb, c, offb_ref, False), ms)

    lv = lamv_ref[...]
    lam = (jnp.exp(jnp.sum(lv[0:1] * lv[1:2], axis=-1, keepdims=True))
           - jnp.exp(jnp.sum(lv[2:3] * lv[3:4], axis=-1, keepdims=True)) + lam_init)
    outs = []
    for h in range(N_HEADS):
        o = _normalized(acc_ref, 2 * h) - lam * _normalized(acc_ref, 2 * h + 1)
        outs.append(_rms(o, g_ref[...]) * (1.0 - lam_init))
    o_ref[...] = jnp.concatenate(outs, axis=-1).astype(o_ref.dtype)


def _attn_a(main3, kt, vpad, offb, diagb, slopes, lamv, g, lam_init):
    b, s_len, _ = main3.shape
    nq = s_len // TQ
    return pl.pallas_call(
        functools.partial(_attn_a_kernel, lam_init=lam_init),
        grid=(b, nq),
        in_specs=[
            pl.BlockSpec((None, TQ, 256), lambda i, j: (i, j, 0)),
            pl.BlockSpec((None, 256, s_len), lambda i, j: (i, 0, 0)),
            pl.BlockSpec((None, s_len, 512), lambda i, j: (i, 0, 0)),
            pl.BlockSpec((N_HEADS, TQ, TQ), lambda i, j: (0, 0, 0)),
            pl.BlockSpec((N_HEADS, TQ, TQ), lambda i, j: (0, 0, 0)),
            pl.BlockSpec(memory_space=pltpu.SMEM),
            pl.BlockSpec((4, DIFF_D), lambda i, j: (0, 0)),
            pl.BlockSpec((1, 2 * DIFF_D), lambda i, j: (0, 0)),
        ],
        out_specs=pl.BlockSpec((None, TQ, 256), lambda i, j: (i, j, 0)),
        out_shape=jax.ShapeDtypeStruct((b, s_len, 256), BF16),
        scratch_shapes=[pltpu.VMEM((2 * N_HEADS, TQ, LANES), F32)],
        compiler_params=_cparams(("arbitrary", "arbitrary")),
        name="attn_diff",
    )(main3, kt, vpad, offb, diagb, slopes, lamv, g)


def _mla_prep_kernel(side_ref, gq_ref, gkv_ref, wqa_ref, wqb_ref, wk_ref, wv_ref, e1_ref, e2_ref,
                     cos_ref, sin_ref, q_ref, k_ref, v_ref, *, qscale):
    side = side_ref[...]
    hq = _rms(side[:, :Q_LORA], gq_ref[...]).astype(BF16)
    hkv = _rms(side[:, Q_LORA:Q_LORA + KV_LORA], gkv_ref[...]).astype(BF16)
    misc = side[:, Q_LORA + KV_LORA:]
    cos4 = jnp.concatenate([cos_ref[...]] * N_HEADS, axis=-1)
    sin4 = jnp.concatenate([sin_ref[...]] * N_HEADS, axis=-1)
    lane = lax.broadcasted_iota(I32, (1, N_HEADS * LANES), 1) % LANES
    nope = (lane < QK_NOPE).astype(F32)
    ones_col = (lane == HEAD_DIM).astype(F32)
    q = (_dot(hq, wqa_ref[...]) * (nope + cos4) + _dot(hq, wqb_ref[...]) * sin4) * qscale
    q_ref[...] = q.astype(BF16)
    hi = misc.astype(BF16)
    lo = (misc - hi.astype(F32)).astype(BF16)
    kr = _dot(hi, e1_ref[...]) + _dot(lo, e1_ref[...])
    krp = _dot(hi, e2_ref[...]) + _dot(lo, e2_ref[...])
    k = _dot(hkv, wk_ref[...]) + kr * cos4 + krp * sin4
    k_ref[...] = k.astype(BF16)
    v_ref[...] = (_dot(hkv, wv_ref[...]) + ones_col).astype(BF16)


def _mla_prep(side, gq, gkv, wqa, wqb, wk, wv, e1, e2, cos_t, sin_t, s_len, qscale):
    n = side.shape[0]
    tm = TM_PROJ
    tpb = s_len // tm
    full = lambda a: pl.BlockSpec(a.shape, lambda i: (0,) * a.ndim)
    out = jax.ShapeDtypeStruct((n, N_HEADS * LANES), BF16)
    return pl.pallas_call(
        functools.partial(_mla_prep_kernel, qscale=qscale),
        grid=(n // tm,),
        in_specs=[pl.BlockSpec((tm, N_SIDE), lambda i: (i, 0)),
                  full(gq), full(gkv), full(wqa), full(wqb), full(wk), full(wv), full(e1), full(e2),
                  pl.BlockSpec((tm, LANES), lambda i: (i % tpb, 0)),
                  pl.BlockSpec((tm, LANES), lambda i: (i % tpb, 0))],
        out_specs=[pl.BlockSpec((tm, N_HEADS * LANES), lambda i: (i, 0))] * 3,
        out_shape=[out, out, out],
        compiler_params=_cparams(("arbitrary",)),
        name="mla_prep",
    )(side, gq, gkv, wqa, wqb, wk, wv, e1, e2, cos_t, sin_t)


def _attn_b_kernel(q_ref, kt_ref, v_ref, diagm_ref, o_ref, acc_ref):
    qb = pl.program_id(1)
    tq = q_ref.shape[0]
    q = q_ref[...]
    qs = [q[:, h * LANES:(h + 1) * LANES] for h in range(N_HEADS)]
    acc_ref[...] = jnp.zeros_like(acc_ref)

    def block(kb, ms, diag):
        k0 = pl.multiple_of(kb * tq, tq)
        out = []
        for h in range(N_HEADS):
            kt = kt_ref[h * LANES:(h + 1) * LANES, pl.ds(k0, tq)]
            v = v_ref[pl.ds(k0, tq), h * LANES:(h + 1) * LANES]
            s = _dot(qs[h], kt)
            if diag:
                s = s + diagm_ref[...]
            out.append(_softmax_update(s, ms[h], acc_ref, h, v))
        return tuple(out)

    ms = tuple(jnp.full((tq, 1), NEG, F32) for _ in range(N_HEADS))
    ms = block(qb, ms, True)
    lax.fori_loop(0, qb, lambda kb, c: block(kb, c, False), ms)
    o_ref[...] = jnp.concatenate([_normalized(acc_ref, h) for h in range(N_HEADS)],
                                 axis=-1).astype(o_ref.dtype)


def _attn_b(q3, kt, vpad, diagm):
    b, s_len, _ = q3.shape
    nq = s_len // TQ
    return pl.pallas_call(
        _attn_b_kernel,
        grid=(b, nq),
        in_specs=[
            pl.BlockSpec((None, TQ, 512), lambda i, j: (i, j, 0)),
            pl.BlockSpec((None, 512, s_len), lambda i, j: (i, 0, 0)),
            pl.BlockSpec((None, s_len, 512), lambda i, j: (i, 0, 0)),
            pl.BlockSpec((TQ, TQ), lambda i, j: (0, 0)),
        ],
        out_specs=pl.BlockSpec((None, TQ, 256), lambda i, j: (i, j, 0)),
        out_shape=jax.ShapeDtypeStruct((b, s_len, 256), BF16),
        scratch_shapes=[pltpu.VMEM((N_HEADS, TQ, LANES), F32)],
        compiler_params=_cparams(("arbitrary", "arbitrary")),
        name="attn_latent",
    )(q3, kt, vpad, diagm)


def _tile_lanes(a, reps):
    return a if reps == 1 else jnp.concatenate([a] * reps, axis=-1)


def _attn_c_kernel(q_ref, qi_ref, w_ref, kt_ref, kit_ref, v_ref, offb_ref, diagb_ref, diagm_ref,
                   slope_ref, tri_ref, o_ref, keys_ref, acc_ref, *, n_sel):
    qb = pl.program_id(1)
    tq = q_ref.shape[0]
    reps = tq // LANES
    nkb = qb + 1
    ones_kn = jnp.ones((tq, LANES), BF16)

    qi = qi_ref[...]
    qis = [qi[:, h * IDX_DIM:(h + 1) * IDX_DIM] for h in range(IDX_HEADS)]
    w = w_ref[...] * (IDX_HEADS ** -0.5 * IDX_DIM ** -0.5)
    ws = [w[:, h:h + 1] for h in range(IDX_HEADS)]

    def score_block(kb, diag):
        k0 = pl.multiple_of(kb * tq, tq)
        kit = kit_ref[:, pl.ds(k0, tq)]
        sc = jnp.zeros((tq, tq), F32)
        for h in range(IDX_HEADS):
            sc = sc + jnp.maximum(_dot(qis[h], kit), 0.0) * ws[h]
        sc = jnp.where(sc == 0.0, 0.0, sc)
        bits = lax.bitcast_convert_type(sc, I32)
        key = bits ^ ((bits >> 31) & 0x7FFFFFFF)
        if diag:
            key = jnp.where(diagm_ref[...] < 0.0, INT_MIN, key)
        keys_ref[:, pl.ds(k0, tq)] = key

    score_block(qb, True)

    def _score_loop(kb, c):
        score_block(kb, False)
        return c
    lax.fori_loop(0, qb, _score_loop, 0)

    def count_ge(cand):
        cand_t = _tile_lanes(cand, reps)

        def body(kb, cnt):
            k0 = pl.multiple_of(kb * tq, tq)
            msk = jnp.where(keys_ref[:, pl.ds(k0, tq)] >= cand_t, 1.0, 0.0).astype(BF16)
            return cnt + _dot(msk, ones_kn)
        return lax.fori_loop(0, nkb, body, jnp.zeros((tq, LANES), F32))

    kf = float(n_sel)
    zero = jnp.zeros((tq, LANES), I32)
    thr = jnp.where(count_ge(zero) >= kf, zero, INT_MIN)

    def bit_body(i, thr):
        cand = thr + (jnp.int32(1) << (30 - i))
        return jnp.where(count_ge(cand) >= kf, cand, thr)
    thr = lax.fori_loop(0, 31, bit_body, thr)
    thr = jnp.maximum(thr, INT_MIN + 1)
    c_gt = count_ge(thr + 1)
    c_ge = count_ge(thr)
    need = kf - c_gt
    has_ties = jnp.max(c_ge) > kf

    q = q_ref[...]
    qs = [q[:, h * HEAD_DIM:(h + 1) * HEAD_DIM] for h in range(N_HEADS)]
    thr_t = _tile_lanes(thr, reps)
    need_t = _tile_lanes(need, reps)

    def attend(kb, carry, bias_ref, diag, ties):
        ms, eq_before = carry
        k0 = pl.multiple_of(kb * tq, tq)
        key = keys_ref[:, pl.ds(k0, tq)]
        if ties:
            eq = jnp.where(key == thr_t, 1.0, 0.0).astype(BF16)
            rank = _dot(eq, tri_ref[...]) + _tile_lanes(eq_before, reps)
            sel = (key > thr_t) | ((key == thr_t) & (rank <= need_t))
            eq_before = eq_before + _dot(eq, ones_kn)
        else:
            sel = key >= thr_t
        out = []
        for h in range(N_HEADS):
            kt = kt_ref[h * HEAD_DIM:(h + 1) * HEAD_DIM, pl.ds(k0, tq)]
            v = v_ref[pl.ds(k0, tq), h * LANES:(h + 1) * LANES]
            s = _dot(qs[h], kt) + bias_ref[h]
            if not diag:
                s = s - slope_ref[h] * ((qb - kb) * tq).astype(F32)
            s = jnp.where(sel, s, NEG)
            out.append(_softmax_update(s, ms[h], acc_ref, h, v))
        return tuple(out), eq_before

    def run(ties):
        acc_ref[...] = jnp.zeros_like(acc_ref)
        carry = (tuple(jnp.full((tq, 1), NEG, F32) for _ in range(N_HEADS)),
                 jnp.zeros((tq, LANES), F32))
        carry = lax.fori_loop(0, qb, lambda kb, c: attend(kb, c, offb_ref, False, ties), carry)
        attend(qb, carry, diagb_ref, True, ties)

    @pl.when(has_ties)
    def _():
        run(True)

    @pl.when(jnp.logical_not(has_ties))
    def _():
        run(False)

    o_ref[...] = jnp.concatenate([_normalized(acc_ref, h) for h in range(N_HEADS)],
                                 axis=-1).astype(o_ref.dtype)


def _attn_c(main3, w3, kt, kit, vpad, offb, diagb, diagm, slopes, tri, n_sel):
    b, s_len, _ = main3.shape
    nq = s_len // TQ
    return pl.pallas_call(
        functools.partial(_attn_c_kernel, n_sel=n_sel),
        grid=(b, nq),
        in_specs=[
            pl.BlockSpec((None, TQ, 256), lambda i, j: (i, j, 3)),
            pl.BlockSpec((None, TQ, 256), lambda i, j: (i, j, 6)),
            pl.BlockSpec((None, TQ, IDX_HEADS), lambda i, j: (i, j, 0)),
            pl.BlockSpec((None, 256, s_len), lambda i, j: (i, 0, 0)),
            pl.BlockSpec((None, IDX_DIM, s_len), lambda i, j: (i, 0, 0)),
            pl.BlockSpec((None, s_len, 512), lambda i, j: (i, 0, 0)),
            pl.BlockSpec((N_HEADS, TQ, TQ), lambda i, j: (0, 0, 0)),
            pl.BlockSpec((N_HEADS, TQ, TQ), lambda i, j: (0, 0, 0)),
            pl.BlockSpec((TQ, TQ), lambda i, j: (0, 0)),
            pl.BlockSpec(memory_space=pltpu.SMEM),
            pl.BlockSpec((TQ, TQ), lambda i, j: (0, 0)),
        ],
        out_specs=pl.BlockSpec((None, TQ, 256), lambda i, j: (i, j, 0)),
        out_shape=jax.ShapeDtypeStruct((b, s_len, 256), BF16),
        scratch_shapes=[pltpu.VMEM((TQ, s_len), I32),
                        pltpu.VMEM((N_HEADS, TQ, LANES), F32)],
        compiler_params=_cparams(("arbitrary", "arbitrary")),
        name="attn_sparse",
    )(main3, main3, w3, kt, kit, vpad, offb, diagb, diagm, slopes, tri)


def _attn_d_kernel(q_ref, kt_ref, v_ref, bias_ref, o_ref):
    qb = pl.program_id(1)
    tq = q_ref.shape[0]
    q = q_ref[...]
    outs = []
    for h in range(N_HEADS):
        qh = q[:, h * HEAD_DIM:(h + 1) * HEAD_DIM]
        ss = []
        for j in range(3):
            k0 = pl.multiple_of((qb + j) * tq, tq)
            kt = kt_ref[h * HEAD_DIM:(h + 1) * HEAD_DIM, pl.ds(k0, tq)]
            s = _dot(qh, kt) + bias_ref[h, :, j * tq:(j + 1) * tq]
            ss.append(jnp.where(qb + j >= 2, s, NEG))
        m = jnp.maximum(jnp.maximum(jnp.max(ss[0], axis=-1, keepdims=True),
                                    jnp.max(ss[1], axis=-1, keepdims=True)),
                        jnp.max(ss[2], axis=-1, keepdims=True))
        acc = jnp.zeros((tq, LANES), F32)
        for j in range(3):
            k0 = pl.multiple_of((qb + j) * tq, tq)
            v = v_ref[pl.ds(k0, tq), h * LANES:(h + 1) * LANES]
            acc = acc + _dot(jnp.exp2(ss[j] - m).astype(BF16), v)
        outs.append(acc[:, :HEAD_DIM] / acc[:, HEAD_DIM:HEAD_DIM + 1])
    o_ref[...] = jnp.concatenate(outs, axis=-1).astype(o_ref.dtype)


def _attn_d(main3, kt_pad, v_pad, bias):
    b, s_len, _ = main3.shape
    nq = s_len // TQ
    sp = kt_pad.shape[-1]
    return pl.pallas_call(
        _attn_d_kernel,
        grid=(b, nq),
        in_specs=[
            pl.BlockSpec((None, TQ, 256), lambda i, j: (i, j, 7)),
            pl.BlockSpec((None, 256, sp), lambda i, j: (i, 0, 0)),
            pl.BlockSpec((None, sp, 512), lambda i, j: (i, 0, 0)),
            pl.BlockSpec((N_HEADS, TQ, 3 * TQ), lambda i, j: (0, 0, 0)),
        ],
        out_specs=pl.BlockSpec((None, TQ, 256), lambda i, j: (i, j, 0)),
        out_shape=jax.ShapeDtypeStruct((b, s_len, 256), BF16),
        compiler_params=_cparams(("arbitrary", "arbitrary")),
        name="attn_band",
    )(main3, kt_pad, v_pad, bias)


def _outproj_kernel(x_ref, oa_ref, ob_ref, oc_ref, od_ref, w_ref, gate_ref, o_ref):
    y = _dot(oa_ref[...], w_ref[0:256, :])
    y = y + _dot(ob_ref[...], w_ref[256:512, :])
    y = y + _dot(oc_ref[...], w_ref[512:768, :])
    y = y + _dot(od_ref[...], w_ref[768:1024, :])
    o_ref[...] = x_ref[...] + gate_ref[...] * y


def _outproj(x2, oa, ob, oc, od, w, mod3, s_len):
    n, d = x2.shape
    tm = TM_PROJ
    tpb = s_len // tm
    ospec = pl.BlockSpec((tm, 256), lambda i: (i, 0))
    return pl.pallas_call(
        _outproj_kernel,
        grid=(n // tm,),
        in_specs=[pl.BlockSpec((tm, d), lambda i: (i, 0)), ospec, ospec, ospec, ospec,
                  pl.BlockSpec((d, d), lambda i: (0, 0)),
                  pl.BlockSpec((None, 1, d), lambda i: (i // tpb, 0, 2))],
        out_specs=pl.BlockSpec((tm, d), lambda i: (i, 0)),
        out_shape=jax.ShapeDtypeStruct((n, d), F32),
        compiler_params=_cparams(("arbitrary",)),
        name="outproj",
    )(x2, oa, ob, oc, od, w, mod3)


def _ffn_kernel(x_ref, g_ref, sh_ref, sc_ref, gate_ref, w1_ref, w3_ref, w2_ref, o_ref,
                hn_ref, acc_ref):
    f = pl.program_id(1)

    @pl.when(f == 0)
    def _():
        hn_ref[...] = _adaln(x_ref[...], g_ref[...], sh_ref[...], sc_ref[...]).astype(BF16)
        acc_ref[...] = jnp.zeros_like(acc_ref)

    hn = hn_ref[...]
    a = _silu(_dot(hn, w1_ref[...])) * _dot(hn, w3_ref[...])
    acc_ref[...] += _dot(a.astype(BF16), w2_ref[...])

    @pl.when(f == pl.num_programs(1) - 1)
    def _():
        o_ref[...] = x_ref[...] + gate_ref[...] * acc_ref[...]


def _ffn(x2, g, mod3, w1, w3, w2, s_len):
    n, d = x2.shape
    dff = w1.shape[1]
    tm = TM_FFN
    tf = dff // 2
    tpb = s_len // tm
    return pl.pallas_call(
        _ffn_kernel,
        grid=(n // tm, dff // tf),
        in_specs=[
            pl.BlockSpec((tm, d), lambda i, f: (i, 0)),
            pl.BlockSpec((1, d), lambda i, f: (0, 0)),
            pl.BlockSpec((None, 1, d), lambda i, f: (i // tpb, 0, 3)),
            pl.BlockSpec((None, 1, d), lambda i, f: (i // tpb, 0, 4)),
            pl.BlockSpec((None, 1, d), lambda i, f: (i // tpb, 0, 5)),
            pl.BlockSpec((d, tf), lambda i, f: (0, f)),
            pl.BlockSpec((d, tf), lambda i, f: (0, f)),
            pl.BlockSpec((tf, d), lambda i, f: (f, 0)),
        ],
        out_specs=pl.BlockSpec((tm, d), lambda i, f: (i, 0)),
        out_shape=jax.ShapeDtypeStruct((n, d), F32),
        scratch_shapes=[pltpu.VMEM((tm, d), BF16), pltpu.VMEM((tm, d), F32)],
        compiler_params=_cparams(("arbitrary", "arbitrary")),
        name="ffn_dense",
    )(x2, g, mod3, mod3, mod3, w1, w3, w2)


def _router_kernel(x_ref, g_ref, sh_ref, sc_ref, rhi_ref, rlo_ref, comb_ref):
    hn = _adaln(x_ref[...], g_ref[...], sh_ref[...], sc_ref[...])
    hi = hn.astype(BF16)
    lo = (hn - hi.astype(F32)).astype(BF16)
    logits = _dot(hi, rhi_ref[...]) + (_dot(hi, rlo_ref[...]) + _dot(lo, rhi_ref[...]))
    lane = lax.broadcasted_iota(I32, logits.shape, 1)
    logits = jnp.where(lane < N_EXPERTS, logits, NEG)
    m1 = jnp.max(logits, axis=-1, keepdims=True)
    i1 = jnp.min(jnp.where(logits == m1, lane, LANES), axis=-1, keepdims=True)
    rest = jnp.where(lane == i1, NEG, logits)
    m2 = jnp.max(rest, axis=-1, keepdims=True)
    i2 = jnp.min(jnp.where(rest == m2, lane, LANES), axis=-1, keepdims=True)
    e2 = jnp.exp(m2 - m1)
    g1 = 1.0 / (1.0 + e2)
    g2 = e2 / (1.0 + e2)
    comb_ref[...] = jnp.where(lane == i1, g1, 0.0) + jnp.where(lane == i2, g2, 0.0)


def _router(x2, g, mod3, rhi, rlo, s_len):
    n, d = x2.shape
    tm = TM_PROJ
    tpb = s_len // tm
    return pl.pallas_call(
        _router_kernel,
        grid=(n // tm,),
        in_specs=[
            pl.BlockSpec((tm, d), lambda i: (i, 0)),
            pl.BlockSpec((1, d), lambda i: (0, 0)),
            pl.BlockSpec((None, 1, d), lambda i: (i // tpb, 0, 3)),
            pl.BlockSpec((None, 1, d), lambda i: (i // tpb, 0, 4)),
            pl.BlockSpec((d, LANES), lambda i: (0, 0)),
            pl.BlockSpec((d, LANES), lambda i: (0, 0)),
        ],
        out_specs=pl.BlockSpec((tm, LANES), lambda i: (i, 0)),
        out_shape=jax.ShapeDtypeStruct((n, LANES), F32),
        compiler_params=_cparams(("arbitrary",)),
        name="router",
    )(x2, g, mod3, mod3, rhi, rlo)


def _moe_kernel(x_ref, g_ref, sh_ref, sc_ref, gate_ref, comb_ref, w1_ref, w3_ref, w2_ref, o_ref,
                hn_ref, acc_ref, eacc_ref):
    e = pl.program_id(1)
    f = pl.program_id(2)
    nf = pl.num_programs(2)

    @pl.when((e == 0) & (f == 0))
    def _():
        hn_ref[...] = _adaln(x_ref[...], g_ref[...], sh_ref[...], sc_ref[...]).astype(BF16)
        acc_ref[...] = jnp.zeros_like(acc_ref)

    @pl.when(f == 0)
    def _():
        eacc_ref[...] = jnp.zeros_like(eacc_ref)

    hn = hn_ref[...]
    a = _silu(_dot(hn, w1_ref[...])) * _dot(hn, w3_ref[...])
    eacc_ref[...] += _dot(a.astype(BF16), w2_ref[...])

    @pl.when(f == nf - 1)
    def _():
        comb = comb_ref[...]
        lane = lax.broadcasted_iota(I32, comb.shape, 1)
        ce = jnp.sum(jnp.where(lane == e, comb, 0.0), axis=-1, keepdims=True)
        acc_ref[...] += ce * eacc_ref[...]

    @pl.when((e == pl.num_programs(1) - 1) & (f == nf - 1))
    def _():
        o_ref[...] = x_ref[...] + gate_ref[...] * acc_ref[...]


def _moe(x2, g, mod3, comb, w1, w3, w2, s_len):
    n, d = x2.shape
    ne, _, dff = w1.shape
    tm = TM_FFN
    tf = dff // 2
    tpb = s_len // tm
    return pl.pallas_call(
        _moe_kernel,
        grid=(n // tm, ne, dff // tf),
        in_specs=[
            pl.BlockSpec((tm, d), lambda i, e, f: (i, 0)),
            pl.BlockSpec((1, d), lambda i, e, f: (0, 0)),
            pl.BlockSpec((None, 1, d), lambda i, e, f: (i // tpb, 0, 3)),
            pl.BlockSpec((None, 1, d), lambda i, e, f: (i // tpb, 0, 4)),
            pl.BlockSpec((None, 1, d), lambda i, e, f: (i // tpb, 0, 5)),
            pl.BlockSpec((tm, LANES), lambda i, e, f: (i, 0)),
            pl.BlockSpec((None, d, tf), lambda i, e, f: (e, 0, f)),
            pl.BlockSpec((None, d, tf), lambda i, e, f: (e, 0, f)),
            pl.BlockSpec((None, tf, d), lambda i, e, f: (e, f, 0)),
        ],
        out_specs=pl.BlockSpec((tm, d), lambda i, e, f: (i, 0)),
        out_shape=jax.ShapeDtypeStruct((n, d), F32),
        scratch_shapes=[pltpu.VMEM((tm, d), BF16), pltpu.VMEM((tm, d), F32),
                        pltpu.VMEM((tm, d), F32)],
        compiler_params=_cparams(("arbitrary", "arbitrary", "arbitrary")),
        name="ffn_experts",
    )(x2, g, mod3, mod3, mod3, comb, w1, w3, w2)


def _final_kernel(x_ref, g_ref, o_ref):
    o_ref[...] = _rms(x_ref[...], g_ref[...])


def _final_norm(x2, g):
    n, d = x2.shape
    tm = TM_FFN
    return pl.pallas_call(
        _final_kernel,
        grid=(n // tm,),
        in_specs=[pl.BlockSpec((tm, d), lambda i: (i, 0)), pl.BlockSpec((1, d), lambda i: (0, 0))],
        out_specs=pl.BlockSpec((tm, d), lambda i: (i, 0)),
        out_shape=jax.ShapeDtypeStruct((n, d), F32),
        compiler_params=_cparams(("arbitrary",)),
        name="final_norm",
    )(x2, g)


def _alibi_slopes():
    return 2.0 ** (-8.0 * jnp.arange(1, N_HEADS + 1, dtype=F32) / N_HEADS)


def _block_tables():
    i = jnp.arange(TQ)[:, None]
    j = jnp.arange(TQ)[None, :]
    sl = (_alibi_slopes() * LOG2E)[:, None, None]
    diagm = jnp.where((j // CHUNK) <= (i // CHUNK), 0.0, NEG).astype(F32)
    offb = -sl * (i - j).astype(F32)[None]
    diagb_nomask = -sl * jnp.abs(i - j).astype(F32)[None]
    return offb, diagb_nomask, diagm


def _rope_tables(s_len):
    inv = ROPE_THETA ** (-jnp.arange(0, QK_ROPE, 2, dtype=F32) / QK_ROPE)
    ang = jnp.arange(s_len, dtype=F32)[:, None] * inv[None, :]
    cos2 = jnp.concatenate([jnp.cos(ang)] * 2, axis=-1)
    sin2 = jnp.concatenate([jnp.sin(ang)] * 2, axis=-1)
    z = jnp.zeros((s_len, QK_NOPE), F32)
    z2 = jnp.zeros((s_len, LANES - QK_NOPE - QK_ROPE), F32)
    return (jnp.concatenate([z, cos2, z2], axis=-1), jnp.concatenate([z, sin2, z2], axis=-1))


def _rot_cols(w):
    half = QK_ROPE // 2
    return jnp.concatenate([-w[..., half:], w[..., :half]], axis=-1)


def _pad_v(v, b, s_len):
    v4 = v.reshape(b, s_len, N_HEADS, HEAD_DIM)
    one = jnp.ones((b, s_len, N_HEADS, 1), v.dtype)
    zero = jnp.zeros((b, s_len, N_HEADS, LANES - HEAD_DIM - 1), v.dtype)
    return jnp.concatenate([v4, one, zero], axis=-1).reshape(b, s_len, N_HEADS * LANES)


def _layer_weights(l, w_in, mla_w_uq, mla_w_ukv):
    w = w_in[l]
    names = ('a_q', 'a_k', 'a_v', 'b_qd', 'b_kvd', 'b_kr', 'c_q', 'c_k', 'c_v', 'c_qi', 'c_ki',
             'c_wi', 'd_q', 'd_k', 'd_v')
    widths = (256, 256, 256, Q_LORA, KV_LORA, QK_ROPE, 256, 256, 256, IDX_HEADS * IDX_DIM, IDX_DIM,
              IDX_HEADS, 256, 256, 256)
    cols, o = {}, 0
    for nme, wd in zip(names, widths):
        cols[nme] = w[:, o:o + wd]
        o += wd
    pad = jnp.zeros((w.shape[0], N_SIDE - (Q_LORA + KV_LORA + 2 * QK_ROPE + IDX_DIM + IDX_HEADS)), F32)
    wcat = jnp.concatenate(
        [cols['a_q'], cols['a_k'], cols['a_v'], cols['c_q'], cols['c_k'], cols['c_v'], cols['c_qi'],
         cols['d_q'], cols['d_k'], cols['d_v'],
         cols['b_qd'], cols['b_kvd'], cols['b_kr'], _rot_cols(cols['b_kr']), cols['c_ki'],
         cols['c_wi'], pad], axis=-1).astype(BF16)
    ones = lambda k: jnp.ones((k,), F32)
    cs = jnp.concatenate([
        ones(256) * (DIFF_D ** -0.5 * LOG2E), ones(512),
        ones(256) * (HEAD_DIM ** -0.5 * LOG2E), ones(768),
        ones(256) * (HEAD_DIM ** -0.5 * LOG2E), ones(512), ones(N_SIDE)])[None, :]

    uq = mla_w_uq[l].reshape(Q_LORA, N_HEADS, QK_NOPE + QK_ROPE)
    zq = jnp.zeros((Q_LORA, N_HEADS, LANES - QK_NOPE - QK_ROPE), F32)
    wqa = jnp.concatenate([uq, zq], axis=-1).reshape(Q_LORA, N_HEADS * LANES).astype(BF16)
    wqb = jnp.concatenate([jnp.zeros((Q_LORA, N_HEADS, QK_NOPE), F32), _rot_cols(uq[..., QK_NOPE:]), zq],
                          axis=-1).reshape(Q_LORA, N_HEADS * LANES).astype(BF16)
    ukv = mla_w_ukv[l].reshape(KV_LORA, N_HEADS, QK_NOPE + HEAD_DIM)
    zk = jnp.zeros((KV_LORA, N_HEADS, LANES - QK_NOPE), F32)
    wk = jnp.concatenate([ukv[..., :QK_NOPE], zk], axis=-1).reshape(KV_LORA, N_HEADS * LANES).astype(BF16)
    wv = jnp.concatenate([ukv[..., QK_NOPE:], zk], axis=-1).reshape(KV_LORA, N_HEADS * LANES).astype(BF16)
    return wcat, cs, wqa, wqb, wk, wv


def _placement():
    r = jnp.arange(LANES)[:, None]
    c = jnp.arange(N_HEADS * LANES)[None, :] % LANES
    e1 = ((c >= QK_NOPE) & (c < QK_NOPE + QK_ROPE) & (r == c - QK_NOPE)).astype(BF16)
    e2 = ((c >= QK_NOPE) & (c < QK_NOPE + QK_ROPE) & (r == c - QK_NOPE + QK_ROPE)).astype(BF16)
    return e1, e2


def _band_bias(rel_bias):
    i = jnp.arange(TQ)[:, None] + 2 * TQ
    j = jnp.arange(3 * TQ)[None, :]
    rel = i - j
    bias = rel_bias[:, jnp.clip(rel, -REL_CLIP, REL_CLIP) + REL_CLIP].astype(F32) * LOG2E
    cq, ck = i // CHUNK, j // CHUNK
    valid = (ck <= cq) & (ck >= cq - BAND_CHUNKS)
    return jnp.where(valid[None], bias, NEG)


def kernel(x, c, ada_w, ada_b, mix_norm_g, ffn_norm_g, w_in, w_out, diff_lambda, diff_norm_g, mla_q_norm_g, mla_kv_norm_g, mla_w_uq, mla_w_ukv, band_rel_bias, ffn_w1, ffn_w3, ffn_w2, moe_router, moe_w1, moe_w3, moe_w2, final_norm_g):
    b, s_len, d = x.shape
    depth = ada_w.shape[0]
    n = b * s_len
    n_sel = min(TOPK_MAX, s_len // 4)
    assert d == D_MODEL and s_len % TM_FFN == 0

    mod = _modulation(c, ada_w, ada_b)
    offb, diagb_nomask, diagm = _block_tables()
    diagb = diagb_nomask + diagm[None]
    slopes = _alibi_slopes() * LOG2E
    cos_t, sin_t = _rope_tables(s_len)
    e1, e2 = _placement()
    tri = (jnp.arange(TQ)[:, None] <= jnp.arange(TQ)[None, :]).astype(BF16)
    mla_scale = (QK_NOPE + QK_ROPE) ** -0.5 * LOG2E
    tr = lambda a: jnp.swapaxes(a, 1, 2)

    x2 = x.reshape(n, d)
    for l in range(depth):
        mod3 = mod[l].reshape(b, 1, 6 * d)
        wcat, cs, wqa, wqb, wk, wv = _layer_weights(l, w_in, mla_w_uq, mla_w_ukv)
        main, side = _inproj(x2, mix_norm_g[l][None], mod3, wcat, cs, s_len)
        main3 = main.reshape(b, s_len, N_MAIN)
        blk = lambda k: main3[:, :, k * 256:(k + 1) * 256]

        lam_init = 0.8 - 0.6 * math.exp(-0.3 * l)
        o_a = _attn_a(main3, tr(blk(1)), _pad_v(blk(2), b, s_len), offb, diagb, slopes,
                      diff_lambda[l], diff_norm_g[l][None], lam_init)

        qb_, kb_, vb_ = _mla_prep(side, mla_q_norm_g[l][None], mla_kv_norm_g[l][None], wqa, wqb, wk, wv,
                                  e1, e2, cos_t, sin_t, s_len, mla_scale)
        o_b = _attn_b(qb_.reshape(b, s_len, -1), tr(kb_.reshape(b, s_len, -1)),
                      vb_.reshape(b, s_len, -1), diagm)

        side3 = side.reshape(b, s_len, N_SIDE)
        o0 = Q_LORA + KV_LORA + 2 * QK_ROPE
        kit = tr(side3[:, :, o0:o0 + IDX_DIM].astype(BF16))
        w3 = side3[:, :, o0 + IDX_DIM:o0 + IDX_DIM + IDX_HEADS]
        o_c = _attn_c(main3, w3, tr(blk(4)), kit, _pad_v(blk(5), b, s_len), offb, diagb_nomask, diagm,
                      slopes, tri, n_sel)

        padk = jnp.pad(tr(blk(8)), ((0, 0), (0, 0), (2 * TQ, 0)))
        padv = jnp.pad(_pad_v(blk(9), b, s_len), ((0, 0), (2 * TQ, 0), (0, 0)))
        o_d = _attn_d(main3, padk, padv, _band_bias(band_rel_bias[l]))

        flat = lambda a: a.reshape(n, 256)
        x2 = _outproj(x2, flat(o_a), flat(o_b), flat(o_c), flat(o_d), w_out[l].astype(BF16), mod3, s_len)

        gf = ffn_norm_g[l][None]
        if l % 2 == 0:
            i = l // 2
            x2 = _ffn(x2, gf, mod3, ffn_w1[i].astype(BF16), ffn_w3[i].astype(BF16),
                      ffn_w2[i].astype(BF16), s_len)
        else:
            i = l // 2
            r = jnp.pad(moe_router[i], ((0, 0), (0, LANES - N_EXPERTS)))
            rhi = r.astype(BF16)
            rlo = (r - rhi.astype(F32)).astype(BF16)
            comb = _router(x2, gf, mod3, rhi, rlo, s_len)
            x2 = _moe(x2, gf, mod3, comb, moe_w1[i].astype(BF16), moe_w3[i].astype(BF16),
                      moe_w2[i].astype(BF16), s_len)
    return _final_norm(x2, final_norm_g[None]).reshape(b, s_len, d)
```

```python
import functools
import math

import numpy as np
import jax
import jax.numpy as jnp
from jax import lax
from jax.experimental import pallas as pl
from jax.experimental.pallas import tpu as pltpu

F32 = jnp.float32
BF16 = jnp.bfloat16
I32 = jnp.int32

D_MODEL = 1024
CHUNK = 64
N_HEADS = 4
HEAD_DIM = 64
DIFF_D = 32
Q_LORA = 256
KV_LORA = 128
QK_NOPE = 64
QK_ROPE = 32
ROPE_THETA = 10000.0
IDX_HEADS = 8
IDX_DIM = 32
TOPK_MAX = 256
BAND_CHUNKS = 8
REL_CLIP = 128
N_EXPERTS = 8
EPS = 1e-6

NEG = -1e30
INT_MIN = -(2 ** 31)

LANES = 128
VMEM_LIMIT = 56 * 1024 * 1024
TQ = 256
TM_PROJ = 512
TM_FFN = 512
N_MAIN = 2560
N_SIDE = 512


def _cparams(sem):
    return pltpu.CompilerParams(dimension_semantics=sem, vmem_limit_bytes=VMEM_LIMIT)


def _dot(a, b):
    return jnp.dot(a, b, preferred_element_type=F32)


def _silu(x):
    return x / (1.0 + jnp.exp(-x))


def _rms(x, g):
    return x * lax.rsqrt(jnp.mean(x * x, axis=-1, keepdims=True) + EPS) * g


def _adaln(x, g, shift, scale):
    return _rms(x, g) * (1.0 + scale) + shift


def _mod_kernel(c_ref, w_ref, b_ref, o_ref):
    sc = _silu(c_ref[...]).astype(BF16)
    o_ref[...] = _dot(sc, w_ref[...].astype(BF16)) + b_ref[...]


def _modulation(c, ada_w, ada_b):
    depth, d, n6 = ada_w.shape
    b = c.shape[0]
    tn = 1536
    return pl.pallas_call(
        _mod_kernel,
        grid=(depth, n6 // tn),
        in_specs=[
            pl.BlockSpec((b, d), lambda l, j: (0, 0)),
            pl.BlockSpec((None, d, tn), lambda l, j: (l, 0, j)),
            pl.BlockSpec((None, 1, tn), lambda l, j: (l, 0, j)),
        ],
        out_specs=pl.BlockSpec((None, b, tn), lambda l, j: (l, 0, j)),
        out_shape=jax.ShapeDtypeStruct((depth, b, n6), F32),
        compiler_params=_cparams(("arbitrary", "arbitrary")),
        name="modulation",
    )(c, ada_w, ada_b.reshape(depth, 1, n6))


def _inproj_kernel(x_ref, g_ref, sh_ref, sc_ref, w_ref, cs_ref, main_ref, side_ref):
    hn = _adaln(x_ref[...], g_ref[...], sh_ref[...], sc_ref[...]).astype(BF16)
    step = 512
    for c0 in range(0, N_MAIN + N_SIDE, step):
        acc = _dot(hn, w_ref[:, c0:c0 + step]) * cs_ref[:, c0:c0 + step]
        if c0 < N_MAIN:
            main_ref[:, c0:c0 + step] = acc.astype(BF16)
        else:
            side_ref[:, c0 - N_MAIN:c0 - N_MAIN + step] = acc


def _inproj(x2, g, mod3, w, cs, s_len):
    n, d = x2.shape
    tm = TM_PROJ
    tpb = s_len // tm
    nc = N_MAIN + N_SIDE
    return pl.pallas_call(
        _inproj_kernel,
        grid=(n // tm,),
        in_specs=[
            pl.BlockSpec((tm, d), lambda i: (i, 0)),
            pl.BlockSpec((1, d), lambda i: (0, 0)),
            pl.BlockSpec((None, 1, d), lambda i: (i // tpb, 0, 0)),
            pl.BlockSpec((None, 1, d), lambda i: (i // tpb, 0, 1)),
            pl.BlockSpec((d, nc), lambda i: (0, 0)),
            pl.BlockSpec((1, nc), lambda i: (0, 0)),
        ],
        out_specs=[
            pl.BlockSpec((tm, N_MAIN), lambda i: (i, 0)),
            pl.BlockSpec((tm, N_SIDE), lambda i: (i, 0)),
        ],
        out_shape=[
            jax.ShapeDtypeStruct((n, N_MAIN), BF16),
            jax.ShapeDtypeStruct((n, N_SIDE), F32),
        ],
        compiler_params=_cparams(("arbitrary",)),
        name="inproj",
    )(x2, g, mod3, mod3, w, cs)


def _softmax_update(s, c, m, acc_ref, slot, vt):
    m_new = jnp.maximum(m, jnp.max(s, axis=0, keepdims=True) + c)
    alpha = jnp.exp(m - m_new)
    p = jnp.exp(s - (m_new - c))
    acc_ref[slot] = acc_ref[slot] * alpha + _dot(vt, p.astype(BF16))
    return m_new


def _normalized(acc_ref, slot):
    a = acc_ref[slot]
    return a[:HEAD_DIM] / a[HEAD_DIM:HEAD_DIM + 1]


def _pipelined(n, produce, consume, lookahead=3):
    vals = [produce(i) for i in range(min(lookahead, n))]
    for i in range(n):
        if i + lookahead < n:
            vals.append(produce(i + lookahead))
        consume(i, vals[i])
        vals[i] = None


def _block_const(slope_ref, h, qb, kb, tq):
    return -slope_ref[h] * ((qb - kb) * tq).astype(F32)


def _attn_a_kernel(qt_ref, k_ref, vt_ref, corr_ref, slope_ref, lamv_ref, g_ref, o_ref, acc_ref,
                   *, lam_init):
    qb = pl.program_id(1)
    tq = o_ref.shape[0]
    nhm = 2 * N_HEADS
    acc_ref[...] = jnp.zeros_like(acc_ref)

    def block(kb, ms, diag):
        k0 = pl.multiple_of(kb * tq, tq)
        out = [None] * nhm

        def logits(i):
            return _dot(k_ref[pl.ds(k0, tq), i * LANES:(i + 1) * LANES], qt_ref[i * LANES:(i + 1) * LANES, :])

        def update(i, s):
            h = i // 2
            vt = vt_ref[h * LANES:(h + 1) * LANES, pl.ds(k0, tq)]
            if diag:
                s, c = s + corr_ref[h], 0.0
            else:
                c = _block_const(slope_ref, h, qb, kb, tq)
            out[i] = _softmax_update(s, c, ms[i], acc_ref, i, vt)

        _pipelined(nhm, logits, update)
        return tuple(out)

    ms = tuple(jnp.full((1, tq), NEG, F32) for _ in range(nhm))
    ms = block(qb, ms, True)
    lax.fori_loop(0, qb, lambda kb, c: block(kb, c, False), ms)

    lv = lamv_ref[...]
    lam = (jnp.exp(jnp.sum(lv[0:1] * lv[1:2], axis=-1, keepdims=True))
           - jnp.exp(jnp.sum(lv[2:3] * lv[3:4], axis=-1, keepdims=True)) + lam_init)
    outs = []
    for h in range(N_HEADS):
        o = _normalized(acc_ref, 2 * h) - lam * _normalized(acc_ref, 2 * h + 1)
        o = o * lax.rsqrt(jnp.mean(o * o, axis=0, keepdims=True) + EPS) * g_ref[...]
        outs.append(o * (1.0 - lam_init))
    o_ref[...] = jnp.transpose(jnp.concatenate(outs, axis=0)).astype(o_ref.dtype)


def _attn_a(qt, k, vt, corr, slopes, lamv, gcol, lam_init):
    b, s_len, _ = k.shape
    nq = s_len // TQ
    return pl.pallas_call(
        functools.partial(_attn_a_kernel, lam_init=lam_init),
        grid=(b, nq),
        in_specs=[
            pl.BlockSpec((None, 2 * N_HEADS * LANES, TQ), lambda i, j: (i, 0, j)),
            pl.BlockSpec((None, s_len, 2 * N_HEADS * LANES), lambda i, j: (i, 0, 0)),
            pl.BlockSpec((None, N_HEADS * LANES, s_len), lambda i, j: (i, 0, 0)),
            pl.BlockSpec((N_HEADS, TQ, TQ), lambda i, j: (0, 0, 0)),
            pl.BlockSpec(memory_space=pltpu.SMEM),
            pl.BlockSpec((4, DIFF_D), lambda i, j: (0, 0)),
            pl.BlockSpec((2 * DIFF_D, TQ), lambda i, j: (0, 0)),
        ],
        out_specs=pl.BlockSpec((None, TQ, 256), lambda i, j: (i, j, 0)),
        out_shape=jax.ShapeDtypeStruct((b, s_len, 256), BF16),
        scratch_shapes=[pltpu.VMEM((2 * N_HEADS, LANES, TQ), F32)],
        compiler_params=_cparams(("arbitrary", "arbitrary")),
        name="attn_diff",
    )(qt, k, vt, corr, slopes, lamv, gcol)


def _mla_prep_kernel(side_ref, gq_ref, gkv_ref, wqa_ref, wqb_ref, wk_ref, wv_ref, e1_ref, e2_ref,
                     cos_ref, sin_ref, q_ref, k_ref, v_ref, *, qscale):
    side = side_ref[...]
    hq = _rms(side[:, :Q_LORA], gq_ref[...]).astype(BF16)
    hkv = _rms(side[:, Q_LORA:Q_LORA + KV_LORA], gkv_ref[...]).astype(BF16)
    misc = side[:, Q_LORA + KV_LORA:]
    cos4 = jnp.concatenate([cos_ref[...]] * N_HEADS, axis=-1)
    sin4 = jnp.concatenate([sin_ref[...]] * N_HEADS, axis=-1)
    lane = lax.broadcasted_iota(I32, (1, N_HEADS * LANES), 1) % LANES
    nope = (lane < QK_NOPE).astype(F32)
    ones_col = (lane == HEAD_DIM).astype(F32)
    q = (_dot(hq, wqa_ref[...]) * (nope + cos4) + _dot(hq, wqb_ref[...]) * sin4) * qscale
    q_ref[...] = q.astype(BF16)
    hi = misc.astype(BF16)
    lo = (misc - hi.astype(F32)).astype(BF16)
    kr = _dot(hi, e1_ref[...]) + _dot(lo, e1_ref[...])
    krp = _dot(hi, e2_ref[...]) + _dot(lo, e2_ref[...])
    k = _dot(hkv, wk_ref[...]) + kr * cos4 + krp * sin4
    k_ref[...] = k.astype(BF16)
    v_ref[...] = (_dot(hkv, wv_ref[...]) + ones_col).astype(BF16)


def _mla_prep(side, gq, gkv, wqa, wqb, wk, wv, e1, e2, cos_t, sin_t, s_len, qscale):
    n = side.shape[0]
    tm = TM_PROJ
    tpb = s_len // tm
    full = lambda a: pl.BlockSpec(a.shape, lambda i: (0,) * a.ndim)
    out = jax.ShapeDtypeStruct((n, N_HEADS * LANES), BF16)
    return pl.pallas_call(
        functools.partial(_mla_prep_kernel, qscale=qscale),
        grid=(n // tm,),
        in_specs=[pl.BlockSpec((tm, N_SIDE), lambda i: (i, 0)),
                  full(gq), full(gkv), full(wqa), full(wqb), full(wk), full(wv), full(e1), full(e2),
                  pl.BlockSpec((tm, LANES), lambda i: (i % tpb, 0)),
                  pl.BlockSpec((tm, LANES), lambda i: (i % tpb, 0))],
        out_specs=[pl.BlockSpec((tm, N_HEADS * LANES), lambda i: (i, 0))] * 3,
        out_shape=[out, out, out],
        compiler_params=_cparams(("arbitrary",)),
        name="mla_prep",
    )(side, gq, gkv, wqa, wqb, wk, wv, e1, e2, cos_t, sin_t)


def _attn_b_kernel(qt_ref, k_ref, vt_ref, diagm_ref, o_ref, acc_ref):
    qb = pl.program_id(1)
    tq = o_ref.shape[0]
    acc_ref[...] = jnp.zeros_like(acc_ref)

    def block(kb, ms, diag):
        k0 = pl.multiple_of(kb * tq, tq)
        out = [None] * N_HEADS

        def logits(h):
            return _dot(k_ref[pl.ds(k0, tq), h * LANES:(h + 1) * LANES], qt_ref[h * LANES:(h + 1) * LANES, :])

        def update(h, s):
            vt = vt_ref[h * LANES:(h + 1) * LANES, pl.ds(k0, tq)]
            if diag:
                s = s + diagm_ref[...]
            out[h] = _softmax_update(s, 0.0, ms[h], acc_ref, h, vt)

        _pipelined(N_HEADS, logits, update)
        return tuple(out)

    ms = tuple(jnp.full((1, tq), NEG, F32) for _ in range(N_HEADS))
    ms = block(qb, ms, True)
    lax.fori_loop(0, qb, lambda kb, c: block(kb, c, False), ms)
    o_ref[...] = jnp.transpose(jnp.concatenate([_normalized(acc_ref, h) for h in range(N_HEADS)],
                                               axis=0)).astype(o_ref.dtype)


def _attn_b(qt, k, vt, diagm):
    b, s_len, _ = k.shape
    nq = s_len // TQ
    return pl.pallas_call(
        _attn_b_kernel,
        grid=(b, nq),
        in_specs=[
            pl.BlockSpec((None, N_HEADS * LANES, TQ), lambda i, j: (i, 0, j)),
            pl.BlockSpec((None, s_len, N_HEADS * LANES), lambda i, j: (i, 0, 0)),
            pl.BlockSpec((None, N_HEADS * LANES, s_len), lambda i, j: (i, 0, 0)),
            pl.BlockSpec((TQ, TQ), lambda i, j: (0, 0)),
        ],
        out_specs=pl.BlockSpec((None, TQ, 256), lambda i, j: (i, j, 0)),
        out_shape=jax.ShapeDtypeStruct((b, s_len, 256), BF16),
        scratch_shapes=[pltpu.VMEM((N_HEADS, LANES, TQ), F32)],
        compiler_params=_cparams(("arbitrary", "arbitrary")),
        name="attn_latent",
    )(qt, k, vt, diagm)


def _attn_c_kernel(qt_ref, qit_ref, wt_ref, k_ref, ki_ref, vt_ref, corr_ref, diagm_ref, slope_ref,
                   tril_ref, o_ref, keys_ref, acc_ref, *, n_sel):
    qb = pl.program_id(1)
    tq = o_ref.shape[0]
    nkb = qb + 1

    wt = wt_ref[...] * (IDX_HEADS ** -0.5 * IDX_DIM ** -0.5)

    def score_block(kb, diag):
        k0 = pl.multiple_of(kb * tq, tq)
        ki = ki_ref[pl.ds(k0, tq), :]
        total = [jnp.zeros((tq, tq), F32)]

        def weighted(h, d):
            total[0] = total[0] + jnp.maximum(d, 0.0) * wt[h:h + 1]

        _pipelined(IDX_HEADS, lambda h: _dot(ki, qit_ref[h * IDX_DIM:(h + 1) * IDX_DIM, :]), weighted)
        sc = jnp.where(total[0] == 0.0, 0.0, total[0])
        bits = lax.bitcast_convert_type(sc, I32)
        key = bits ^ ((bits >> 31) & 0x7FFFFFFF)
        if diag:
            key = jnp.where(diagm_ref[...] < 0.0, INT_MIN, key)
        keys_ref[pl.ds(k0, tq), :] = key

    score_block(qb, True)

    def _score_loop(kb, c):
        score_block(kb, False)
        return c
    lax.fori_loop(0, qb, _score_loop, 0)

    def count_ge(cand):
        def body(kb, cnt8):
            k0 = pl.multiple_of(kb * tq, tq)
            hit = jnp.where(keys_ref[pl.ds(k0, tq), :] >= cand, 1, 0)
            return cnt8 + jnp.sum(hit.reshape(tq // 8, 8, tq), axis=0)
        cnt8 = lax.fori_loop(0, nkb, body, jnp.zeros((8, tq), I32))
        return jnp.sum(cnt8, axis=0, keepdims=True)

    zero = jnp.zeros((1, tq), I32)
    thr = jnp.where(count_ge(zero) >= n_sel, zero, INT_MIN)

    def bit_body(i, thr):
        cand = thr + (jnp.int32(1) << (30 - i))
        return jnp.where(count_ge(cand) >= n_sel, cand, thr)
    thr = lax.fori_loop(0, 31, bit_body, thr)
    thr = jnp.maximum(thr, INT_MIN + 1)
    c_gt = count_ge(thr + 1)
    c_ge = count_ge(thr)
    need = (n_sel - c_gt).astype(F32)
    has_ties = jnp.max(c_ge) > n_sel

    def attend(kb, carry, diag, ties):
        ms, eq_before = carry
        k0 = pl.multiple_of(kb * tq, tq)
        key = keys_ref[pl.ds(k0, tq), :]
        if ties:
            eq = jnp.where(key == thr, 1.0, 0.0)
            rank = _dot(tril_ref[...], eq.astype(BF16)) + eq_before
            sel = (key > thr) | ((key == thr) & (rank <= need))
            eq_before = eq_before + jnp.sum(eq, axis=0, keepdims=True)
        else:
            sel = key >= thr
        out = [None] * N_HEADS

        def logits(h):
            return _dot(k_ref[pl.ds(k0, tq), h * LANES:(h + 1) * LANES], qt_ref[h * LANES:(h + 1) * LANES, :])

        def update(h, s):
            vt = vt_ref[h * LANES:(h + 1) * LANES, pl.ds(k0, tq)]
            if diag:
                s, c = s + corr_ref[h], 0.0
            else:
                c = _block_const(slope_ref, h, qb, kb, tq)
            s = jnp.where(sel, s, NEG)
            out[h] = _softmax_update(s, c, ms[h], acc_ref, h, vt)

        _pipelined(N_HEADS, logits, update)
        return tuple(out), eq_before

    def run(ties):
        acc_ref[...] = jnp.zeros_like(acc_ref)
        carry = (tuple(jnp.full((1, tq), NEG, F32) for _ in range(N_HEADS)), jnp.zeros((1, tq), F32))
        carry = lax.fori_loop(0, qb, lambda kb, c: attend(kb, c, False, ties), carry)
        attend(qb, carry, True, ties)

    @pl.when(has_ties)
    def _():
        run(True)

    @pl.when(jnp.logical_not(has_ties))
    def _():
        run(False)

    o_ref[...] = jnp.transpose(jnp.concatenate([_normalized(acc_ref, h) for h in range(N_HEADS)],
                                               axis=0)).astype(o_ref.dtype)


def _attn_c(qt, qit, wt, k, ki, vt, corr, diagm, slopes, tril, n_sel):
    b, s_len, _ = k.shape
    nq = s_len // TQ
    return pl.pallas_call(
        functools.partial(_attn_c_kernel, n_sel=n_sel),
        grid=(b, nq),
        in_specs=[
            pl.BlockSpec((None, N_HEADS * LANES, TQ), lambda i, j: (i, 0, j)),
            pl.BlockSpec((None, IDX_HEADS * IDX_DIM, TQ), lambda i, j: (i, 0, j)),
            pl.BlockSpec((None, IDX_HEADS, TQ), lambda i, j: (i, 0, j)),
            pl.BlockSpec((None, s_len, N_HEADS * LANES), lambda i, j: (i, 0, 0)),
            pl.BlockSpec((None, s_len, IDX_DIM), lambda i, j: (i, 0, 0)),
            pl.BlockSpec((None, N_HEADS * LANES, s_len), lambda i, j: (i, 0, 0)),
            pl.BlockSpec((N_HEADS, TQ, TQ), lambda i, j: (0, 0, 0)),
            pl.BlockSpec((TQ, TQ), lambda i, j: (0, 0)),
            pl.BlockSpec(memory_space=pltpu.SMEM),
            pl.BlockSpec((TQ, TQ), lambda i, j: (0, 0)),
        ],
        out_specs=pl.BlockSpec((None, TQ, 256), lambda i, j: (i, j, 0)),
        out_shape=jax.ShapeDtypeStruct((b, s_len, 256), BF16),
        scratch_shapes=[pltpu.VMEM((s_len, TQ), I32),
                        pltpu.VMEM((N_HEADS, LANES, TQ), F32)],
        compiler_params=_cparams(("arbitrary", "arbitrary")),
        name="attn_sparse",
    )(qt, qit, wt, k, ki, vt, corr, diagm, slopes, tril)


def _band_bias_kernel(rb_ref, o_ref):
    h = pl.program_id(0)
    nk, tq = o_ref.shape
    j = lax.broadcasted_iota(I32, (nk, tq), 0)
    t = lax.broadcasted_iota(I32, (nk, tq), 1) + (nk - tq)
    idx = jnp.clip(t - j, -REL_CLIP, REL_CLIP) + REL_CLIP
    cq, ck = t // CHUNK, j // CHUNK
    valid = (ck <= cq) & (ck >= cq - BAND_CHUNKS)
    tbl = lax.fori_loop(0, 2 * REL_CLIP + 1,
                        lambda r, tb: jnp.where(idx == r, rb_ref[h, r], tb), jnp.zeros((nk, tq), F32))
    o_ref[...] = jnp.where(valid, tbl, NEG)


def _band_bias(rel_bias):
    nh = rel_bias.shape[0]
    return pl.pallas_call(
        _band_bias_kernel,
        grid=(nh,),
        in_specs=[pl.BlockSpec(memory_space=pltpu.SMEM)],
        out_specs=pl.BlockSpec((None, 3 * TQ, TQ), lambda h: (h, 0, 0)),
        out_shape=jax.ShapeDtypeStruct((nh, 3 * TQ, TQ), F32),
        compiler_params=_cparams(("arbitrary",)),
        name="band_bias",
    )(rel_bias)


def _attn_d_kernel(qt_ref, k_ref, vt_ref, bias_ref, o_ref):
    qb = pl.program_id(1)
    tq = o_ref.shape[0]
    outs = [None] * N_HEADS

    def logits(h):
        ds = []
        for j in range(3):
            k0 = pl.multiple_of((qb + j) * tq, tq)
            ds.append(_dot(k_ref[pl.ds(k0, tq), h * LANES:(h + 1) * LANES], qt_ref[h * LANES:(h + 1) * LANES, :]))
        return ds

    def attend(h, ds):
        ss = []
        for j in range(3):
            s = ds[j] + bias_ref[h, j * tq:(j + 1) * tq, :]
            ss.append(jnp.where(qb + j >= 2, s, NEG))
        m = jnp.maximum(jnp.maximum(jnp.max(ss[0], axis=0, keepdims=True),
                                    jnp.max(ss[1], axis=0, keepdims=True)),
                        jnp.max(ss[2], axis=0, keepdims=True))
        acc = jnp.zeros((LANES, tq), F32)
        for j in range(3):
            k0 = pl.multiple_of((qb + j) * tq, tq)
            vt = vt_ref[h * LANES:(h + 1) * LANES, pl.ds(k0, tq)]
            acc = acc + _dot(vt, jnp.exp(ss[j] - m).astype(BF16))
        outs[h] = acc[:HEAD_DIM] / acc[HEAD_DIM:HEAD_DIM + 1]

    _pipelined(N_HEADS, logits, attend, lookahead=1)
    o_ref[...] = jnp.transpose(jnp.concatenate(outs, axis=0)).astype(o_ref.dtype)


def _attn_d(qt, k_pad, vt_pad, bias):
    b, sp, _ = k_pad.shape
    s_len = sp - 2 * TQ
    nq = s_len // TQ
    return pl.pallas_call(
        _attn_d_kernel,
        grid=(b, nq),
        in_specs=[
            pl.BlockSpec((None, N_HEADS * LANES, TQ), lambda i, j: (i, 0, j)),
            pl.BlockSpec((None, sp, N_HEADS * LANES), lambda i, j: (i, 0, 0)),
            pl.BlockSpec((None, N_HEADS * LANES, sp), lambda i, j: (i, 0, 0)),
            pl.BlockSpec((N_HEADS, 3 * TQ, TQ), lambda i, j: (0, 0, 0)),
        ],
        out_specs=pl.BlockSpec((None, TQ, 256), lambda i, j: (i, j, 0)),
        out_shape=jax.ShapeDtypeStruct((b, s_len, 256), BF16),
        compiler_params=_cparams(("arbitrary", "arbitrary")),
        name="attn_band",
    )(qt, k_pad, vt_pad, bias)


def _outproj_kernel(x_ref, oa_ref, ob_ref, oc_ref, od_ref, w_ref, gate_ref, o_ref):
    y = _dot(oa_ref[...], w_ref[0:256, :])
    y = y + _dot(ob_ref[...], w_ref[256:512, :])
    y = y + _dot(oc_ref[...], w_ref[512:768, :])
    y = y + _dot(od_ref[...], w_ref[768:1024, :])
    o_ref[...] = x_ref[...] + gate_ref[...] * y


def _outproj(x2, oa, ob, oc, od, w, mod3, s_len):
    n, d = x2.shape
    tm = TM_PROJ
    tpb = s_len // tm
    ospec = pl.BlockSpec((tm, 256), lambda i: (i, 0))
    return pl.pallas_call(
        _outproj_kernel,
        grid=(n // tm,),
        in_specs=[pl.BlockSpec((tm, d), lambda i: (i, 0)), ospec, ospec, ospec, ospec,
                  pl.BlockSpec((d, d), lambda i: (0, 0)),
                  pl.BlockSpec((None, 1, d), lambda i: (i // tpb, 0, 2))],
        out_specs=pl.BlockSpec((tm, d), lambda i: (i, 0)),
        out_shape=jax.ShapeDtypeStruct((n, d), F32),
        compiler_params=_cparams(("arbitrary",)),
        name="outproj",
    )(x2, oa, ob, oc, od, w, mod3)


def _ffn_kernel(x_ref, g_ref, sh_ref, sc_ref, gate_ref, w1_ref, w3_ref, w2_ref, o_ref,
                hn_ref, acc_ref):
    f = pl.program_id(1)

    @pl.when(f == 0)
    def _():
        hn_ref[...] = _adaln(x_ref[...], g_ref[...], sh_ref[...], sc_ref[...]).astype(BF16)
        acc_ref[...] = jnp.zeros_like(acc_ref)

    hn = hn_ref[...]
    a = _silu(_dot(hn, w1_ref[...])) * _dot(hn, w3_ref[...])
    acc_ref[...] += _dot(a.astype(BF16), w2_ref[...])

    @pl.when(f == pl.num_programs(1) - 1)
    def _():
        o_ref[...] = x_ref[...] + gate_ref[...] * acc_ref[...]


def _ffn(x2, g, mod3, w1, w3, w2, s_len):
    n, d = x2.shape
    dff = w1.shape[1]
    tm = TM_FFN
    tf = dff // 2
    tpb = s_len // tm
    return pl.pallas_call(
        _ffn_kernel,
        grid=(n // tm, dff // tf),
        in_specs=[
            pl.BlockSpec((tm, d), lambda i, f: (i, 0)),
            pl.BlockSpec((1, d), lambda i, f: (0, 0)),
            pl.BlockSpec((None, 1, d), lambda i, f: (i // tpb, 0, 3)),
            pl.BlockSpec((None, 1, d), lambda i, f: (i // tpb, 0, 4)),
            pl.BlockSpec((None, 1, d), lambda i, f: (i // tpb, 0, 5)),
            pl.BlockSpec((d, tf), lambda i, f: (0, f)),
            pl.BlockSpec((d, tf), lambda i, f: (0, f)),
            pl.BlockSpec((tf, d), lambda i, f: (f, 0)),
        ],
        out_specs=pl.BlockSpec((tm, d), lambda i, f: (i, 0)),
        out_shape=jax.ShapeDtypeStruct((n, d), F32),
        scratch_shapes=[pltpu.VMEM((tm, d), BF16), pltpu.VMEM((tm, d), F32)],
        compiler_params=_cparams(("arbitrary", "arbitrary")),
        name="ffn_dense",
    )(x2, g, mod3, mod3, mod3, w1, w3, w2)


def _router_kernel(x_ref, g_ref, sh_ref, sc_ref, rhi_ref, rlo_ref, comb_ref):
    hn = _adaln(x_ref[...], g_ref[...], sh_ref[...], sc_ref[...])
    hi = hn.astype(BF16)
    lo = (hn - hi.astype(F32)).astype(BF16)
    logits = _dot(hi, rhi_ref[...]) + (_dot(hi, rlo_ref[...]) + _dot(lo, rhi_ref[...]))
    lane = lax.broadcasted_iota(I32, logits.shape, 1)
    logits = jnp.where(lane < N_EXPERTS, logits, NEG)
    m1 = jnp.max(logits, axis=-1, keepdims=True)
    i1 = jnp.min(jnp.where(logits == m1, lane, LANES), axis=-1, keepdims=True)
    rest = jnp.where(lane == i1, NEG, logits)
    m2 = jnp.max(rest, axis=-1, keepdims=True)
    i2 = jnp.min(jnp.where(rest == m2, lane, LANES), axis=-1, keepdims=True)
    e2 = jnp.exp(m2 - m1)
    g1 = 1.0 / (1.0 + e2)
    g2 = e2 / (1.0 + e2)
    comb_ref[...] = jnp.where(lane == i1, g1, 0.0) + jnp.where(lane == i2, g2, 0.0)


def _router(x2, g, mod3, rhi, rlo, s_len):
    n, d = x2.shape
    tm = TM_PROJ
    tpb = s_len // tm
    return pl.pallas_call(
        _router_kernel,
        grid=(n // tm,),
        in_specs=[
            pl.BlockSpec((tm, d), lambda i: (i, 0)),
            pl.BlockSpec((1, d), lambda i: (0, 0)),
            pl.BlockSpec((None, 1, d), lambda i: (i // tpb, 0, 3)),
            pl.BlockSpec((None, 1, d), lambda i: (i // tpb, 0, 4)),
            pl.BlockSpec((d, LANES), lambda i: (0, 0)),
            pl.BlockSpec((d, LANES), lambda i: (0, 0)),
        ],
        out_specs=pl.BlockSpec((tm, LANES), lambda i: (i, 0)),
        out_shape=jax.ShapeDtypeStruct((n, LANES), F32),
        compiler_params=_cparams(("arbitrary",)),
        name="router",
    )(x2, g, mod3, mod3, rhi, rlo)


def _moe_kernel(x_ref, g_ref, sh_ref, sc_ref, gate_ref, comb_ref, w1_ref, w3_ref, w2_ref, o_ref,
                hn_ref, acc_ref, eacc_ref):
    e = pl.program_id(1)
    f = pl.program_id(2)
    nf = pl.num_programs(2)

    @pl.when((e == 0) & (f == 0))
    def _():
        hn_ref[...] = _adaln(x_ref[...], g_ref[...], sh_ref[...], sc_ref[...]).astype(BF16)
        acc_ref[...] = jnp.zeros_like(acc_ref)

    @pl.when(f == 0)
    def _():
        eacc_ref[...] = jnp.zeros_like(eacc_ref)

    hn = hn_ref[...]
    a = _silu(_dot(hn, w1_ref[...])) * _dot(hn, w3_ref[...])
    eacc_ref[...] += _dot(a.astype(BF16), w2_ref[...])

    @pl.when(f == nf - 1)
    def _():
        comb = comb_ref[...]
        lane = lax.broadcasted_iota(I32, comb.shape, 1)
        ce = jnp.sum(jnp.where(lane == e, comb, 0.0), axis=-1, keepdims=True)
        acc_ref[...] += ce * eacc_ref[...]

    @pl.when((e == pl.num_programs(1) - 1) & (f == nf - 1))
    def _():
        o_ref[...] = x_ref[...] + gate_ref[...] * acc_ref[...]


def _moe(x2, g, mod3, comb, w1, w3, w2, s_len):
    n, d = x2.shape
    ne, _, dff = w1.shape
    tm = TM_FFN
    tf = dff // 2
    tpb = s_len // tm
    return pl.pallas_call(
        _moe_kernel,
        grid=(n // tm, ne, dff // tf),
        in_specs=[
            pl.BlockSpec((tm, d), lambda i, e, f: (i, 0)),
            pl.BlockSpec((1, d), lambda i, e, f: (0, 0)),
            pl.BlockSpec((None, 1, d), lambda i, e, f: (i // tpb, 0, 3)),
            pl.BlockSpec((None, 1, d), lambda i, e, f: (i // tpb, 0, 4)),
            pl.BlockSpec((None, 1, d), lambda i, e, f: (i // tpb, 0, 5)),
            pl.BlockSpec((tm, LANES), lambda i, e, f: (i, 0)),
            pl.BlockSpec((None, d, tf), lambda i, e, f: (e, 0, f)),
            pl.BlockSpec((None, d, tf), lambda i, e, f: (e, 0, f)),
            pl.BlockSpec((None, tf, d), lambda i, e, f: (e, f, 0)),
        ],
        out_specs=pl.BlockSpec((tm, d), lambda i, e, f: (i, 0)),
        out_shape=jax.ShapeDtypeStruct((n, d), F32),
        scratch_shapes=[pltpu.VMEM((tm, d), BF16), pltpu.VMEM((tm, d), F32),
                        pltpu.VMEM((tm, d), F32)],
        compiler_params=_cparams(("arbitrary", "arbitrary", "arbitrary")),
        name="ffn_experts",
    )(x2, g, mod3, mod3, mod3, comb, w1, w3, w2)


def _final_kernel(x_ref, g_ref, o_ref):
    o_ref[...] = _rms(x_ref[...], g_ref[...])


def _final_norm(x2, g):
    n, d = x2.shape
    tm = TM_FFN
    return pl.pallas_call(
        _final_kernel,
        grid=(n // tm,),
        in_specs=[pl.BlockSpec((tm, d), lambda i: (i, 0)), pl.BlockSpec((1, d), lambda i: (0, 0))],
        out_specs=pl.BlockSpec((tm, d), lambda i: (i, 0)),
        out_shape=jax.ShapeDtypeStruct((n, d), F32),
        compiler_params=_cparams(("arbitrary",)),
        name="final_norm",
    )(x2, g)


def _alibi_slopes():
    return 2.0 ** (-8.0 * jnp.arange(1, N_HEADS + 1, dtype=F32) / N_HEADS)


def _block_tables():
    j = jnp.arange(TQ)[:, None]
    i = jnp.arange(TQ)[None, :]
    diagm = jnp.where((j // CHUNK) <= (i // CHUNK), 0.0, NEG).astype(F32)
    corr = -2.0 * _alibi_slopes()[:, None, None] * jnp.maximum(j - i, 0).astype(F32)[None]
    return corr, diagm


def _rope_tables(s_len):
    inv = ROPE_THETA ** (-jnp.arange(0, QK_ROPE, 2, dtype=F32) / QK_ROPE)
    ang = jnp.arange(s_len, dtype=F32)[:, None] * inv[None, :]
    cos2 = jnp.concatenate([jnp.cos(ang)] * 2, axis=-1)
    sin2 = jnp.concatenate([jnp.sin(ang)] * 2, axis=-1)
    z = jnp.zeros((s_len, QK_NOPE), F32)
    z2 = jnp.zeros((s_len, LANES - QK_NOPE - QK_ROPE), F32)
    return (jnp.concatenate([z, cos2, z2], axis=-1), jnp.concatenate([z, sin2, z2], axis=-1))


def _rot_cols(w):
    half = QK_ROPE // 2
    return jnp.concatenate([-w[..., half:], w[..., :half]], axis=-1)


def _slabs(a, nh, extra=None):
    b, s_len, _ = a.shape
    parts = [a.reshape(b, s_len, nh, -1)]
    if extra is not None:
        parts.append(jnp.broadcast_to(extra, (b, s_len, nh, extra.shape[-1])).astype(a.dtype))
    used = sum(p.shape[-1] for p in parts)
    parts.append(jnp.zeros((b, s_len, nh, LANES - used), a.dtype))
    return jnp.concatenate(parts, axis=-1)


def _rows(slab):
    b, s_len = slab.shape[:2]
    return slab.reshape(b, s_len, -1)


def _cols(slab):
    b, s_len = slab.shape[:2]
    return jnp.transpose(slab, (0, 2, 3, 1)).reshape(b, -1, s_len)


def _alibi_extras(s_len, nslab):
    loc = (jnp.arange(s_len) % TQ).astype(F32)[:, None, None]
    sl = jnp.repeat(_alibi_slopes(), nslab // N_HEADS)[None, :, None]
    q_extra = jnp.concatenate([-sl * loc, jnp.broadcast_to(sl, (s_len, nslab, 1))], axis=-1)
    k_extra = jnp.concatenate([jnp.ones((s_len, nslab, 1), F32),
                               jnp.broadcast_to(loc, (s_len, nslab, 1))], axis=-1)
    return q_extra[None], k_extra[None]


def _layer_weights(l, w_in, mla_w_uq, mla_w_ukv):
    w = w_in[l]
    names = ('a_q', 'a_k', 'a_v', 'b_qd', 'b_kvd', 'b_kr', 'c_q', 'c_k', 'c_v', 'c_qi', 'c_ki',
             'c_wi', 'd_q', 'd_k', 'd_v')
    widths = (256, 256, 256, Q_LORA, KV_LORA, QK_ROPE, 256, 256, 256, IDX_HEADS * IDX_DIM, IDX_DIM,
              IDX_HEADS, 256, 256, 256)
    cols, o = {}, 0
    for nme, wd in zip(names, widths):
        cols[nme] = w[:, o:o + wd]
        o += wd
    pad = jnp.zeros((w.shape[0], N_SIDE - (Q_LORA + KV_LORA + 2 * QK_ROPE + IDX_DIM + IDX_HEADS)), F32)
    wcat = jnp.concatenate(
        [cols['a_q'], cols['a_k'], cols['a_v'], cols['c_q'], cols['c_k'], cols['c_v'], cols['c_qi'],
         cols['d_q'], cols['d_k'], cols['d_v'],
         cols['b_qd'], cols['b_kvd'], cols['b_kr'], _rot_cols(cols['b_kr']), cols['c_ki'],
         cols['c_wi'], pad], axis=-1).astype(BF16)
    ones = lambda k: jnp.ones((k,), F32)
    cs = jnp.concatenate([
        ones(256) * DIFF_D ** -0.5, ones(512),
        ones(256) * HEAD_DIM ** -0.5, ones(768),
        ones(256) * HEAD_DIM ** -0.5, ones(512), ones(N_SIDE)])[None, :]

    uq = mla_w_uq[l].reshape(Q_LORA, N_HEADS, QK_NOPE + QK_ROPE)
    zq = jnp.zeros((Q_LORA, N_HEADS, LANES - QK_NOPE - QK_ROPE), F32)
    wqa = jnp.concatenate([uq, zq], axis=-1).reshape(Q_LORA, N_HEADS * LANES).astype(BF16)
    wqb = jnp.concatenate([jnp.zeros((Q_LORA, N_HEADS, QK_NOPE), F32), _rot_cols(uq[..., QK_NOPE:]), zq],
                          axis=-1).reshape(Q_LORA, N_HEADS * LANES).astype(BF16)
    ukv = mla_w_ukv[l].reshape(KV_LORA, N_HEADS, QK_NOPE + HEAD_DIM)
    zk = jnp.zeros((KV_LORA, N_HEADS, LANES - QK_NOPE), F32)
    wk = jnp.concatenate([ukv[..., :QK_NOPE], zk], axis=-1).reshape(KV_LORA, N_HEADS * LANES).astype(BF16)
    wv = jnp.concatenate([ukv[..., QK_NOPE:], zk], axis=-1).reshape(KV_LORA, N_HEADS * LANES).astype(BF16)
    return wcat, cs, wqa, wqb, wk, wv


def _placement():
    r = jnp.arange(LANES)[:, None]
    c = jnp.arange(N_HEADS * LANES)[None, :] % LANES
    e1 = ((c >= QK_NOPE) & (c < QK_NOPE + QK_ROPE) & (r == c - QK_NOPE)).astype(BF16)
    e2 = ((c >= QK_NOPE) & (c < QK_NOPE + QK_ROPE) & (r == c - QK_NOPE + QK_ROPE)).astype(BF16)
    return e1, e2


def kernel(x, c, ada_w, ada_b, mix_norm_g, ffn_norm_g, w_in, w_out, diff_lambda, diff_norm_g, mla_q_norm_g, mla_kv_norm_g, mla_w_uq, mla_w_ukv, band_rel_bias, ffn_w1, ffn_w3, ffn_w2, moe_router, moe_w1, moe_w3, moe_w2, final_norm_g):
    b, s_len, d = x.shape
    depth = ada_w.shape[0]
    n = b * s_len
    n_sel = min(TOPK_MAX, s_len // 4)
    assert d == D_MODEL and s_len % TM_FFN == 0 and s_len % TQ == 0

    mod = _modulation(c, ada_w, ada_b)
    corr, diagm = _block_tables()
    slopes = _alibi_slopes()
    cos_t, sin_t = _rope_tables(s_len)
    e1, e2 = _placement()
    tril = (jnp.arange(TQ)[None, :] <= jnp.arange(TQ)[:, None]).astype(BF16)
    mla_scale = (QK_NOPE + QK_ROPE) ** -0.5
    qx8, kx8 = _alibi_extras(s_len, 2 * N_HEADS)
    qx4, kx4 = _alibi_extras(s_len, N_HEADS)
    one4 = jnp.ones((1, s_len, N_HEADS, 1), F32)
    tr = lambda a: jnp.swapaxes(a, 1, 2)
    front = 2 * TQ

    x2 = x.reshape(n, d)
    for l in range(depth):
        mod3 = mod[l].reshape(b, 1, 6 * d)
        wcat, cs, wqa, wqb, wk, wv = _layer_weights(l, w_in, mla_w_uq, mla_w_ukv)
        main, side = _inproj(x2, mix_norm_g[l][None], mod3, wcat, cs, s_len)
        main3 = main.reshape(b, s_len, N_MAIN)
        side3 = side.reshape(b, s_len, N_SIDE)
        blk = lambda k: main3[:, :, k * 256:(k + 1) * 256]

        lam_init = 0.8 - 0.6 * math.exp(-0.3 * l)
        o_a = _attn_a(_cols(_slabs(blk(0), 2 * N_HEADS, qx8)), _rows(_slabs(blk(1), 2 * N_HEADS, kx8)),
                      _cols(_slabs(blk(2), N_HEADS, one4)), corr + diagm[None], slopes, diff_lambda[l],
                      jnp.broadcast_to(diff_norm_g[l][:, None], (2 * DIFF_D, TQ)), lam_init)

        qb_, kb_, vb_ = _mla_prep(side, mla_q_norm_g[l][None], mla_kv_norm_g[l][None], wqa, wqb, wk, wv,
                                  e1, e2, cos_t, sin_t, s_len, mla_scale)
        o_b = _attn_b(tr(qb_.reshape(b, s_len, -1)), kb_.reshape(b, s_len, -1),
                      tr(vb_.reshape(b, s_len, -1)), diagm)

        o0 = Q_LORA + KV_LORA + 2 * QK_ROPE
        ki = side3[:, :, o0:o0 + IDX_DIM].astype(BF16)
        wt = tr(side3[:, :, o0 + IDX_DIM:o0 + IDX_DIM + IDX_HEADS])
        o_c = _attn_c(_cols(_slabs(blk(3), N_HEADS, qx4)), tr(blk(6)), wt, _rows(_slabs(blk(4), N_HEADS, kx4)),
                      ki, _cols(_slabs(blk(5), N_HEADS, one4)), corr, diagm, slopes, tril, n_sel)

        k_pad = jnp.pad(_rows(_slabs(blk(8), N_HEADS)), ((0, 0), (front, 0), (0, 0)))
        vt_pad = jnp.pad(_cols(_slabs(blk(9), N_HEADS, one4)), ((0, 0), (0, 0), (front, 0)))
        o_d = _attn_d(_cols(_slabs(blk(7), N_HEADS)), k_pad, vt_pad, _band_bias(band_rel_bias[l]))

        flat = lambda a: a.reshape(n, 256)
        x2 = _outproj(x2, flat(o_a), flat(o_b), flat(o_c), flat(o_d), w_out[l].astype(BF16), mod3, s_len)

        gf = ffn_norm_g[l][None]
        if l % 2 == 0:
            i = l // 2
            x2 = _ffn(x2, gf, mod3, ffn_w1[i].astype(BF16), ffn_w3[i].astype(BF16),
                      ffn_w2[i].astype(BF16), s_len)
        else:
            i = l // 2
            r = jnp.pad(moe_router[i], ((0, 0), (0, LANES - N_EXPERTS)))
            rhi = r.astype(BF16)
            rlo = (r - rhi.astype(F32)).astype(BF16)
            comb = _router(x2, gf, mod3, rhi, rlo, s_len)
            x2 = _moe(x2, gf, mod3, comb, moe_w1[i].astype(BF16), moe_w3[i].astype(BF16),
                      moe_w2[i].astype(BF16), s_len)
    return _final_norm(x2, final_norm_g[None]).reshape(b, s_len, d)
```

```python
import functools
import math

import numpy as np
import jax
import jax.numpy as jnp
from jax import lax
from jax.experimental import pallas as pl
from jax.experimental.pallas import tpu as pltpu

F32 = jnp.float32
BF16 = jnp.bfloat16
I32 = jnp.int32

D_MODEL = 1024
CHUNK = 64
N_HEADS = 4
HEAD_DIM = 64
DIFF_D = 32
Q_LORA = 256
KV_LORA = 128
QK_NOPE = 64
QK_ROPE = 32
ROPE_THETA = 10000.0
IDX_HEADS = 8
IDX_DIM = 32
TOPK_MAX = 256
BAND_CHUNKS = 8
REL_CLIP = 128
N_EXPERTS = 8
EPS = 1e-6

NEG = -1e30
INT_MIN = -(2 ** 31)

LANES = 128
VMEM_LIMIT = 56 * 1024 * 1024
TQ = 256
TM_PROJ = 512
TM_FFN = 512
TM_MOE = 2048
MOE_CHUNK = 128
N_MAIN = 2560
N_SIDE = 512


def _cparams(sem):
    return pltpu.CompilerParams(dimension_semantics=sem, vmem_limit_bytes=VMEM_LIMIT)


def _dot(a, b):
    return jnp.dot(a, b, preferred_element_type=F32)


def _silu(x):
    return x / (1.0 + jnp.exp(-x))


def _rms(x, g):
    return x * lax.rsqrt(jnp.mean(x * x, axis=-1, keepdims=True) + EPS) * g


def _adaln(x, g, shift, scale):
    return _rms(x, g) * (1.0 + scale) + shift


def _mod_kernel(c_ref, w_ref, b_ref, o_ref):
    sc = _silu(c_ref[...]).astype(BF16)
    o_ref[...] = _dot(sc, w_ref[...].astype(BF16)) + b_ref[...]


def _modulation(c, ada_w, ada_b):
    depth, d, n6 = ada_w.shape
    b = c.shape[0]
    tn = 1536
    return pl.pallas_call(
        _mod_kernel,
        grid=(depth, n6 // tn),
        in_specs=[
            pl.BlockSpec((b, d), lambda l, j: (0, 0)),
            pl.BlockSpec((None, d, tn), lambda l, j: (l, 0, j)),
            pl.BlockSpec((None, 1, tn), lambda l, j: (l, 0, j)),
        ],
        out_specs=pl.BlockSpec((None, b, tn), lambda l, j: (l, 0, j)),
        out_shape=jax.ShapeDtypeStruct((depth, b, n6), F32),
        compiler_params=_cparams(("arbitrary", "arbitrary")),
        name="modulation",
    )(c, ada_w, ada_b.reshape(depth, 1, n6))


def _inproj_kernel(x_ref, g_ref, sh_ref, sc_ref, w_ref, cs_ref, main_ref, side_ref):
    hn = _adaln(x_ref[...], g_ref[...], sh_ref[...], sc_ref[...]).astype(BF16)
    step = 512
    for c0 in range(0, N_MAIN + N_SIDE, step):
        acc = _dot(hn, w_ref[:, c0:c0 + step]) * cs_ref[:, c0:c0 + step]
        if c0 < N_MAIN:
            main_ref[:, c0:c0 + step] = acc.astype(BF16)
        else:
            side_ref[:, c0 - N_MAIN:c0 - N_MAIN + step] = acc


def _inproj(x2, g, mod3, w, cs, s_len):
    n, d = x2.shape
    tm = TM_PROJ
    tpb = s_len // tm
    nc = N_MAIN + N_SIDE
    return pl.pallas_call(
        _inproj_kernel,
        grid=(n // tm,),
        in_specs=[
            pl.BlockSpec((tm, d), lambda i: (i, 0)),
            pl.BlockSpec((1, d), lambda i: (0, 0)),
            pl.BlockSpec((None, 1, d), lambda i: (i // tpb, 0, 0)),
            pl.BlockSpec((None, 1, d), lambda i: (i // tpb, 0, 1)),
            pl.BlockSpec((d, nc), lambda i: (0, 0)),
            pl.BlockSpec((1, nc), lambda i: (0, 0)),
        ],
        out_specs=[
            pl.BlockSpec((tm, N_MAIN), lambda i: (i, 0)),
            pl.BlockSpec((tm, N_SIDE), lambda i: (i, 0)),
        ],
        out_shape=[
            jax.ShapeDtypeStruct((n, N_MAIN), BF16),
            jax.ShapeDtypeStruct((n, N_SIDE), F32),
        ],
        compiler_params=_cparams(("arbitrary",)),
        name="inproj",
    )(x2, g, mod3, mod3, w, cs)


def _softmax_update(s, c, m, acc_ref, slot, vt):
    m_new = jnp.maximum(m, jnp.max(s, axis=0, keepdims=True) + c)
    alpha = jnp.exp(m - m_new)
    p = jnp.exp(s - (m_new - c))
    acc_ref[slot] = acc_ref[slot] * alpha + _dot(vt, p.astype(BF16))
    return m_new


def _normalized(acc_ref, slot):
    a = acc_ref[slot]
    return a[:HEAD_DIM] / a[HEAD_DIM:HEAD_DIM + 1]


def _pipelined(n, produce, consume, lookahead=3):
    vals = [produce(i) for i in range(min(lookahead, n))]
    for i in range(n):
        if i + lookahead < n:
            vals.append(produce(i + lookahead))
        consume(i, vals[i])
        vals[i] = None


def _block_const(slope_ref, h, qb, kb, tq):
    return -slope_ref[h] * ((qb - kb) * tq).astype(F32)


def _attn_a_kernel(qt_ref, k_ref, vt_ref, corr_ref, slope_ref, lamv_ref, g_ref, o_ref, acc_ref,
                   *, lam_init):
    qb = pl.program_id(1)
    tq = o_ref.shape[0]
    nhm = 2 * N_HEADS
    acc_ref[...] = jnp.zeros_like(acc_ref)

    def block(kb, ms, diag):
        k0 = pl.multiple_of(kb * tq, tq)
        out = [None] * nhm

        def logits(i):
            return _dot(k_ref[pl.ds(k0, tq), i * LANES:(i + 1) * LANES], qt_ref[i * LANES:(i + 1) * LANES, :])

        def update(i, s):
            h = i // 2
            vt = vt_ref[h * LANES:(h + 1) * LANES, pl.ds(k0, tq)]
            if diag:
                s, c = corr_ref[h] + s, 0.0
            else:
                c = _block_const(slope_ref, h, qb, kb, tq)
            out[i] = _softmax_update(s, c, ms[i], acc_ref, i, vt)

        _pipelined(nhm, logits, update)
        return tuple(out)

    ms = tuple(jnp.full((1, tq), NEG, F32) for _ in range(nhm))
    ms = block(qb, ms, True)
    lax.fori_loop(0, qb, lambda kb, c: block(kb, c, False), ms)

    lv = lamv_ref[...]
    lam = (jnp.exp(jnp.sum(lv[0:1] * lv[1:2], axis=-1, keepdims=True))
           - jnp.exp(jnp.sum(lv[2:3] * lv[3:4], axis=-1, keepdims=True)) + lam_init)
    outs = []
    for h in range(N_HEADS):
        o = _normalized(acc_ref, 2 * h) - lam * _normalized(acc_ref, 2 * h + 1)
        o = o * lax.rsqrt(jnp.mean(o * o, axis=0, keepdims=True) + EPS) * g_ref[...]
        outs.append(o * (1.0 - lam_init))
    o_ref[...] = jnp.transpose(jnp.concatenate(outs, axis=0)).astype(o_ref.dtype)


def _attn_a(qt, k, vt, corr, slopes, lamv, gcol, lam_init):
    b, s_len, _ = k.shape
    nq = s_len // TQ
    return pl.pallas_call(
        functools.partial(_attn_a_kernel, lam_init=lam_init),
        grid=(b, nq),
        in_specs=[
            pl.BlockSpec((None, 2 * N_HEADS * LANES, TQ), lambda i, j: (i, 0, j)),
            pl.BlockSpec((None, s_len, 2 * N_HEADS * LANES), lambda i, j: (i, 0, 0)),
            pl.BlockSpec((None, N_HEADS * LANES, s_len), lambda i, j: (i, 0, 0)),
            pl.BlockSpec((N_HEADS, TQ, TQ), lambda i, j: (0, 0, 0)),
            pl.BlockSpec(memory_space=pltpu.SMEM),
            pl.BlockSpec((4, DIFF_D), lambda i, j: (0, 0)),
            pl.BlockSpec((2 * DIFF_D, TQ), lambda i, j: (0, 0)),
        ],
        out_specs=pl.BlockSpec((None, TQ, 256), lambda i, j: (i, j, 0)),
        out_shape=jax.ShapeDtypeStruct((b, s_len, 256), BF16),
        scratch_shapes=[pltpu.VMEM((2 * N_HEADS, LANES, TQ), F32)],
        compiler_params=_cparams(("arbitrary", "arbitrary")),
        name="attn_diff",
    )(qt, k, vt, corr, slopes, lamv, gcol)


def _mla_prep_kernel(side_ref, gq_ref, gkv_ref, wqa_ref, wqb_ref, wk_ref, wv_ref, e1_ref, e2_ref,
                     cos_ref, sin_ref, q_ref, k_ref, v_ref, *, qscale):
    side = side_ref[...]
    hq = _rms(side[:, :Q_LORA], gq_ref[...]).astype(BF16)
    hkv = _rms(side[:, Q_LORA:Q_LORA + KV_LORA], gkv_ref[...]).astype(BF16)
    misc = side[:, Q_LORA + KV_LORA:]
    cos4 = jnp.concatenate([cos_ref[...]] * N_HEADS, axis=-1)
    sin4 = jnp.concatenate([sin_ref[...]] * N_HEADS, axis=-1)
    lane = lax.broadcasted_iota(I32, (1, N_HEADS * LANES), 1) % LANES
    nope = (lane < QK_NOPE).astype(F32)
    ones_col = (lane == HEAD_DIM).astype(F32)
    q = (_dot(hq, wqa_ref[...]) * (nope + cos4) + _dot(hq, wqb_ref[...]) * sin4) * qscale
    q_ref[...] = q.astype(BF16)
    hi = misc.astype(BF16)
    lo = (misc - hi.astype(F32)).astype(BF16)
    kr = _dot(hi, e1_ref[...]) + _dot(lo, e1_ref[...])
    krp = _dot(hi, e2_ref[...]) + _dot(lo, e2_ref[...])
    k = _dot(hkv, wk_ref[...]) + kr * cos4 + krp * sin4
    k_ref[...] = k.astype(BF16)
    v_ref[...] = (_dot(hkv, wv_ref[...]) + ones_col).astype(BF16)


def _mla_prep(side, gq, gkv, wqa, wqb, wk, wv, e1, e2, cos_t, sin_t, s_len, qscale):
    n = side.shape[0]
    tm = TM_PROJ
    tpb = s_len // tm
    full = lambda a: pl.BlockSpec(a.shape, lambda i: (0,) * a.ndim)
    out = jax.ShapeDtypeStruct((n, N_HEADS * LANES), BF16)
    return pl.pallas_call(
        functools.partial(_mla_prep_kernel, qscale=qscale),
        grid=(n // tm,),
        in_specs=[pl.BlockSpec((tm, N_SIDE), lambda i: (i, 0)),
                  full(gq), full(gkv), full(wqa), full(wqb), full(wk), full(wv), full(e1), full(e2),
                  pl.BlockSpec((tm, LANES), lambda i: (i % tpb, 0)),
                  pl.BlockSpec((tm, LANES), lambda i: (i % tpb, 0))],
        out_specs=[pl.BlockSpec((tm, N_HEADS * LANES), lambda i: (i, 0))] * 3,
        out_shape=[out, out, out],
        compiler_params=_cparams(("arbitrary",)),
        name="mla_prep",
    )(side, gq, gkv, wqa, wqb, wk, wv, e1, e2, cos_t, sin_t)


def _attn_b_kernel(qt_ref, k_ref, vt_ref, diagm_ref, o_ref, acc_ref):
    qb = pl.program_id(1)
    tq = o_ref.shape[0]
    acc_ref[...] = jnp.zeros_like(acc_ref)

    def block(kb, ms, diag):
        k0 = pl.multiple_of(kb * tq, tq)
        out = [None] * N_HEADS

        def logits(h):
            return _dot(k_ref[pl.ds(k0, tq), h * LANES:(h + 1) * LANES], qt_ref[h * LANES:(h + 1) * LANES, :])

        def update(h, s):
            vt = vt_ref[h * LANES:(h + 1) * LANES, pl.ds(k0, tq)]
            if diag:
                s = diagm_ref[...] + s
            out[h] = _softmax_update(s, 0.0, ms[h], acc_ref, h, vt)

        _pipelined(N_HEADS, logits, update)
        return tuple(out)

    ms = tuple(jnp.full((1, tq), NEG, F32) for _ in range(N_HEADS))
    ms = block(qb, ms, True)
    lax.fori_loop(0, qb, lambda kb, c: block(kb, c, False), ms)
    o_ref[...] = jnp.transpose(jnp.concatenate([_normalized(acc_ref, h) for h in range(N_HEADS)],
                                               axis=0)).astype(o_ref.dtype)


def _attn_b(qt, k, vt, diagm):
    b, s_len, _ = k.shape
    nq = s_len // TQ
    return pl.pallas_call(
        _attn_b_kernel,
        grid=(b, nq),
        in_specs=[
            pl.BlockSpec((None, N_HEADS * LANES, TQ), lambda i, j: (i, 0, j)),
            pl.BlockSpec((None, s_len, N_HEADS * LANES), lambda i, j: (i, 0, 0)),
            pl.BlockSpec((None, N_HEADS * LANES, s_len), lambda i, j: (i, 0, 0)),
            pl.BlockSpec((TQ, TQ), lambda i, j: (0, 0)),
        ],
        out_specs=pl.BlockSpec((None, TQ, 256), lambda i, j: (i, j, 0)),
        out_shape=jax.ShapeDtypeStruct((b, s_len, 256), BF16),
        scratch_shapes=[pltpu.VMEM((N_HEADS, LANES, TQ), F32)],
        compiler_params=_cparams(("arbitrary", "arbitrary")),
        name="attn_latent",
    )(qt, k, vt, diagm)


def _attn_c_kernel(qt_ref, qit_ref, wt_ref, k_ref, ki_ref, vt_ref, corr_ref, diagm_ref, slope_ref,
                   tril_ref, o_ref, keys_ref, acc_ref, *, n_sel):
    qb = pl.program_id(1)
    tq = o_ref.shape[0]
    nkb = qb + 1

    wt = wt_ref[...] * (IDX_HEADS ** -0.5 * IDX_DIM ** -0.5)

    def score_block(kb, diag):
        k0 = pl.multiple_of(kb * tq, tq)
        ki = ki_ref[pl.ds(k0, tq), :]
        total = [jnp.zeros((tq, tq), F32)]

        def weighted(h, d):
            total[0] = total[0] + jnp.maximum(d, 0.0) * wt[h:h + 1]

        _pipelined(IDX_HEADS, lambda h: _dot(ki, qit_ref[h * IDX_DIM:(h + 1) * IDX_DIM, :]), weighted)
        sc = jnp.where(total[0] == 0.0, 0.0, total[0])
        bits = lax.bitcast_convert_type(sc, I32)
        key = bits ^ ((bits >> 31) & 0x7FFFFFFF)
        if diag:
            key = jnp.where(diagm_ref[...] < 0.0, INT_MIN, key)
        keys_ref[pl.ds(k0, tq), :] = key

    score_block(qb, True)

    def _score_loop(kb, c):
        score_block(kb, False)
        return c
    lax.fori_loop(0, qb, _score_loop, 0)

    def count_ge(cand):
        def body(kb, cnt8):
            k0 = pl.multiple_of(kb * tq, tq)
            hit = jnp.where(keys_ref[pl.ds(k0, tq), :] >= cand, 1, 0)
            return cnt8 + jnp.sum(hit.reshape(tq // 8, 8, tq), axis=0)
        cnt8 = lax.fori_loop(0, nkb, body, jnp.zeros((8, tq), I32))
        return jnp.sum(cnt8, axis=0, keepdims=True)

    zero = jnp.zeros((1, tq), I32)
    thr = jnp.where(count_ge(zero) >= n_sel, zero, INT_MIN)

    def bit_body(i, thr):
        cand = thr + (jnp.int32(1) << (30 - i))
        return jnp.where(count_ge(cand) >= n_sel, cand, thr)
    thr = lax.fori_loop(0, 31, bit_body, thr)
    thr = jnp.maximum(thr, INT_MIN + 1)
    c_gt = count_ge(thr + 1)
    c_ge = count_ge(thr)
    need = (n_sel - c_gt).astype(F32)
    has_ties = jnp.max(c_ge) > n_sel

    def attend(kb, carry, diag, ties):
        ms, eq_before = carry
        k0 = pl.multiple_of(kb * tq, tq)
        key = keys_ref[pl.ds(k0, tq), :]
        if ties:
            eq = jnp.where(key == thr, 1.0, 0.0)
            rank = _dot(tril_ref[...], eq.astype(BF16)) + eq_before
            sel = (key > thr) | ((key == thr) & (rank <= need))
            eq_before = eq_before + jnp.sum(eq, axis=0, keepdims=True)
        else:
            sel = key >= thr
        out = [None] * N_HEADS

        def logits(h):
            return _dot(k_ref[pl.ds(k0, tq), h * LANES:(h + 1) * LANES], qt_ref[h * LANES:(h + 1) * LANES, :])

        def update(h, s):
            vt = vt_ref[h * LANES:(h + 1) * LANES, pl.ds(k0, tq)]
            if diag:
                s, c = corr_ref[h] + s, 0.0
            else:
                c = _block_const(slope_ref, h, qb, kb, tq)
            s = jnp.where(sel, s, NEG)
            out[h] = _softmax_update(s, c, ms[h], acc_ref, h, vt)

        _pipelined(N_HEADS, logits, update)
        return tuple(out), eq_before

    def run(ties):
        acc_ref[...] = jnp.zeros_like(acc_ref)
        carry = (tuple(jnp.full((1, tq), NEG, F32) for _ in range(N_HEADS)), jnp.zeros((1, tq), F32))
        carry = lax.fori_loop(0, qb, lambda kb, c: attend(kb, c, False, ties), carry)
        attend(qb, carry, True, ties)

    @pl.when(has_ties)
    def _():
        run(True)

    @pl.when(jnp.logical_not(has_ties))
    def _():
        run(False)

    o_ref[...] = jnp.transpose(jnp.concatenate([_normalized(acc_ref, h) for h in range(N_HEADS)],
                                               axis=0)).astype(o_ref.dtype)


def _attn_c(qt, qit, wt, k, ki, vt, corr, diagm, slopes, tril, n_sel):
    b, s_len, _ = k.shape
    nq = s_len // TQ
    return pl.pallas_call(
        functools.partial(_attn_c_kernel, n_sel=n_sel),
        grid=(b, nq),
        in_specs=[
            pl.BlockSpec((None, N_HEADS * LANES, TQ), lambda i, j: (i, 0, j)),
            pl.BlockSpec((None, IDX_HEADS * IDX_DIM, TQ), lambda i, j: (i, 0, j)),
            pl.BlockSpec((None, IDX_HEADS, TQ), lambda i, j: (i, 0, j)),
            pl.BlockSpec((None, s_len, N_HEADS * LANES), lambda i, j: (i, 0, 0)),
            pl.BlockSpec((None, s_len, IDX_DIM), lambda i, j: (i, 0, 0)),
            pl.BlockSpec((None, N_HEADS * LANES, s_len), lambda i, j: (i, 0, 0)),
            pl.BlockSpec((N_HEADS, TQ, TQ), lambda i, j: (0, 0, 0)),
            pl.BlockSpec((TQ, TQ), lambda i, j: (0, 0)),
            pl.BlockSpec(memory_space=pltpu.SMEM),
            pl.BlockSpec((TQ, TQ), lambda i, j: (0, 0)),
        ],
        out_specs=pl.BlockSpec((None, TQ, 256), lambda i, j: (i, j, 0)),
        out_shape=jax.ShapeDtypeStruct((b, s_len, 256), BF16),
        scratch_shapes=[pltpu.VMEM((s_len, TQ), I32),
                        pltpu.VMEM((N_HEADS, LANES, TQ), F32)],
        compiler_params=_cparams(("arbitrary", "arbitrary")),
        name="attn_sparse",
    )(qt, qit, wt, k, ki, vt, corr, diagm, slopes, tril)


def _band_bias_kernel(rb_ref, o_ref):
    h = pl.program_id(0)
    nk, tq = o_ref.shape
    j = lax.broadcasted_iota(I32, (nk, tq), 0)
    t = lax.broadcasted_iota(I32, (nk, tq), 1) + (nk - tq)
    idx = jnp.clip(t - j, -REL_CLIP, REL_CLIP) + REL_CLIP
    cq, ck = t // CHUNK, j // CHUNK
    valid = (ck <= cq) & (ck >= cq - BAND_CHUNKS)
    tbl = lax.fori_loop(0, 2 * REL_CLIP + 1,
                        lambda r, tb: jnp.where(idx == r, rb_ref[h, r], tb), jnp.zeros((nk, tq), F32))
    o_ref[...] = jnp.where(valid, tbl, NEG)


def _band_bias(rel_bias):
    nh = rel_bias.shape[0]
    return pl.pallas_call(
        _band_bias_kernel,
        grid=(nh,),
        in_specs=[pl.BlockSpec(memory_space=pltpu.SMEM)],
        out_specs=pl.BlockSpec((None, 3 * TQ, TQ), lambda h: (h, 0, 0)),
        out_shape=jax.ShapeDtypeStruct((nh, 3 * TQ, TQ), F32),
        compiler_params=_cparams(("arbitrary",)),
        name="band_bias",
    )(rel_bias)


def _attn_d_kernel(qt_ref, k_ref, vt_ref, bias_ref, o_ref):
    qb = pl.program_id(1)
    tq = o_ref.shape[0]
    outs = [None] * N_HEADS

    def logits(h):
        ds = []
        for j in range(3):
            k0 = pl.multiple_of((qb + j) * tq, tq)
            ds.append(_dot(k_ref[pl.ds(k0, tq), h * LANES:(h + 1) * LANES], qt_ref[h * LANES:(h + 1) * LANES, :]))
        return ds

    def attend(h, ds):
        ss = []
        for j in range(3):
            s = bias_ref[h, j * tq:(j + 1) * tq, :] + ds[j]
            ss.append(jnp.where(qb + j >= 2, s, NEG))
        m = jnp.maximum(jnp.maximum(jnp.max(ss[0], axis=0, keepdims=True),
                                    jnp.max(ss[1], axis=0, keepdims=True)),
                        jnp.max(ss[2], axis=0, keepdims=True))
        acc = jnp.zeros((LANES, tq), F32)
        for j in range(3):
            k0 = pl.multiple_of((qb + j) * tq, tq)
            vt = vt_ref[h * LANES:(h + 1) * LANES, pl.ds(k0, tq)]
            acc = acc + _dot(vt, jnp.exp(ss[j] - m).astype(BF16))
        outs[h] = acc[:HEAD_DIM] / acc[HEAD_DIM:HEAD_DIM + 1]

    _pipelined(N_HEADS, logits, attend, lookahead=1)
    o_ref[...] = jnp.transpose(jnp.concatenate(outs, axis=0)).astype(o_ref.dtype)


def _attn_d(qt, k_pad, vt_pad, bias):
    b, sp, _ = k_pad.shape
    s_len = sp - 2 * TQ
    nq = s_len // TQ
    return pl.pallas_call(
        _attn_d_kernel,
        grid=(b, nq),
        in_specs=[
            pl.BlockSpec((None, N_HEADS * LANES, TQ), lambda i, j: (i, 0, j)),
            pl.BlockSpec((None, sp, N_HEADS * LANES), lambda i, j: (i, 0, 0)),
            pl.BlockSpec((None, N_HEADS * LANES, sp), lambda i, j: (i, 0, 0)),
            pl.BlockSpec((N_HEADS, 3 * TQ, TQ), lambda i, j: (0, 0, 0)),
        ],
        out_specs=pl.BlockSpec((None, TQ, 256), lambda i, j: (i, j, 0)),
        out_shape=jax.ShapeDtypeStruct((b, s_len, 256), BF16),
        compiler_params=_cparams(("arbitrary", "arbitrary")),
        name="attn_band",
    )(qt, k_pad, vt_pad, bias)


def _outproj_kernel(x_ref, oa_ref, ob_ref, oc_ref, od_ref, w_ref, gate_ref, o_ref):
    y = _dot(oa_ref[...], w_ref[0:256, :])
    y = y + _dot(ob_ref[...], w_ref[256:512, :])
    y = y + _dot(oc_ref[...], w_ref[512:768, :])
    y = y + _dot(od_ref[...], w_ref[768:1024, :])
    o_ref[...] = x_ref[...] + gate_ref[...] * y


def _outproj(x2, oa, ob, oc, od, w, mod3, s_len):
    n, d = x2.shape
    tm = TM_PROJ
    tpb = s_len // tm
    ospec = pl.BlockSpec((tm, 256), lambda i: (i, 0))
    return pl.pallas_call(
        _outproj_kernel,
        grid=(n // tm,),
        in_specs=[pl.BlockSpec((tm, d), lambda i: (i, 0)), ospec, ospec, ospec, ospec,
                  pl.BlockSpec((d, d), lambda i: (0, 0)),
                  pl.BlockSpec((None, 1, d), lambda i: (i // tpb, 0, 2))],
        out_specs=pl.BlockSpec((tm, d), lambda i: (i, 0)),
        out_shape=jax.ShapeDtypeStruct((n, d), F32),
        compiler_params=_cparams(("arbitrary",)),
        name="outproj",
    )(x2, oa, ob, oc, od, w, mod3)


def _ffn_kernel(x_ref, g_ref, sh_ref, sc_ref, gate_ref, w1_ref, w3_ref, w2_ref, o_ref,
                hn_ref, acc_ref):
    f = pl.program_id(1)

    @pl.when(f == 0)
    def _():
        hn_ref[...] = _adaln(x_ref[...], g_ref[...], sh_ref[...], sc_ref[...]).astype(BF16)
        acc_ref[...] = jnp.zeros_like(acc_ref)

    hn = hn_ref[...]
    a = _silu(_dot(hn, w1_ref[...])) * _dot(hn, w3_ref[...])
    acc_ref[...] += _dot(a.astype(BF16), w2_ref[...])

    @pl.when(f == pl.num_programs(1) - 1)
    def _():
        o_ref[...] = x_ref[...] + gate_ref[...] * acc_ref[...]


def _ffn(x2, g, mod3, w1, w3, w2, s_len):
    n, d = x2.shape
    dff = w1.shape[1]
    tm = TM_FFN
    tf = dff // 2
    tpb = s_len // tm
    return pl.pallas_call(
        _ffn_kernel,
        grid=(n // tm, dff // tf),
        in_specs=[
            pl.BlockSpec((tm, d), lambda i, f: (i, 0)),
            pl.BlockSpec((1, d), lambda i, f: (0, 0)),
            pl.BlockSpec((None, 1, d), lambda i, f: (i // tpb, 0, 3)),
            pl.BlockSpec((None, 1, d), lambda i, f: (i // tpb, 0, 4)),
            pl.BlockSpec((None, 1, d), lambda i, f: (i // tpb, 0, 5)),
            pl.BlockSpec((d, tf), lambda i, f: (0, f)),
            pl.BlockSpec((d, tf), lambda i, f: (0, f)),
            pl.BlockSpec((tf, d), lambda i, f: (f, 0)),
        ],
        out_specs=pl.BlockSpec((tm, d), lambda i, f: (i, 0)),
        out_shape=jax.ShapeDtypeStruct((n, d), F32),
        scratch_shapes=[pltpu.VMEM((tm, d), BF16), pltpu.VMEM((tm, d), F32)],
        compiler_params=_cparams(("arbitrary", "arbitrary")),
        name="ffn_dense",
    )(x2, g, mod3, mod3, mod3, w1, w3, w2)


def _router_kernel(x_ref, g_ref, sh_ref, sc_ref, rhi_ref, rlo_ref, tril_ref,
                   hn_ref, comb_ref, pos_ref, post_ref, cnt_ref):
    sub = tril_ref.shape[0]
    carry = jnp.zeros((1, LANES), F32)
    for r in range(x_ref.shape[0] // sub):
        rows = slice(r * sub, (r + 1) * sub)
        hn = _adaln(x_ref[rows, :], g_ref[...], sh_ref[...], sc_ref[...])
        hi = hn.astype(BF16)
        hn_ref[rows, :] = hi
        lo = (hn - hi.astype(F32)).astype(BF16)
        logits = _dot(hi, rhi_ref[...]) + (_dot(hi, rlo_ref[...]) + _dot(lo, rhi_ref[...]))
        lane = lax.broadcasted_iota(I32, logits.shape, 1)
        logits = jnp.where(lane < N_EXPERTS, logits, NEG)
        m1 = jnp.max(logits, axis=-1, keepdims=True)
        i1 = jnp.min(jnp.where(logits == m1, lane, LANES), axis=-1, keepdims=True)
        rest = jnp.where(lane == i1, NEG, logits)
        m2 = jnp.max(rest, axis=-1, keepdims=True)
        i2 = jnp.min(jnp.where(rest == m2, lane, LANES), axis=-1, keepdims=True)
        e2 = jnp.exp(m2 - m1)
        g1 = 1.0 / (1.0 + e2)
        g2 = e2 / (1.0 + e2)
        comb_ref[rows, :] = jnp.where(lane == i1, g1, 0.0) + jnp.where(lane == i2, g2, 0.0)
        routed = jnp.where((lane == i1) | (lane == i2), 1.0, 0.0)
        incl = _dot(tril_ref[...], routed.astype(BF16)) + carry
        pos = jnp.where(routed > 0.0, incl - 1.0, -1.0)
        pos_ref[rows, :] = pos
        post_ref[:, rows] = jnp.transpose(pos)[:N_EXPERTS, :]
        carry = carry + jnp.sum(routed, axis=0, keepdims=True)
    cnt_ref[...] = carry


def _router(x2, g, mod3, rhi, rlo, tril, s_len):
    n, d = x2.shape
    tm = min(TM_MOE, s_len)
    tpb = s_len // tm
    ns = n // tm
    return pl.pallas_call(
        _router_kernel,
        grid=(ns,),
        in_specs=[
            pl.BlockSpec((tm, d), lambda i: (i, 0)),
            pl.BlockSpec((1, d), lambda i: (0, 0)),
            pl.BlockSpec((None, 1, d), lambda i: (i // tpb, 0, 3)),
            pl.BlockSpec((None, 1, d), lambda i: (i // tpb, 0, 4)),
            pl.BlockSpec((d, LANES), lambda i: (0, 0)),
            pl.BlockSpec((d, LANES), lambda i: (0, 0)),
            pl.BlockSpec(tril.shape, lambda i: (0, 0)),
        ],
        out_specs=[
            pl.BlockSpec((tm, d), lambda i: (i, 0)),
            pl.BlockSpec((tm, LANES), lambda i: (i, 0)),
            pl.BlockSpec((tm, LANES), lambda i: (i, 0)),
            pl.BlockSpec((None, N_EXPERTS, tm), lambda i: (i, 0, 0)),
            pl.BlockSpec((None, 1, LANES), lambda i: (i, 0, 0)),
        ],
        out_shape=[
            jax.ShapeDtypeStruct((n, d), BF16),
            jax.ShapeDtypeStruct((n, LANES), F32),
            jax.ShapeDtypeStruct((n, LANES), F32),
            jax.ShapeDtypeStruct((ns, N_EXPERTS, tm), F32),
            jax.ShapeDtypeStruct((ns, 1, LANES), F32),
        ],
        compiler_params=_cparams(("arbitrary",)),
        name="router",
    )(x2, g, mod3, mod3, rhi, rlo, tril)


def _moe_kernel(cnt_ref, hn_ref, comb_ref, pos_ref, post_ref, w1_ref, w3_ref, w2_ref, o_ref,
                xc_ref, y_ref):
    s, e, f = pl.program_id(0), pl.program_id(1), pl.program_id(2)
    nf = pl.num_programs(2)
    tm = hn_ref.shape[0]
    ch = MOE_CHUNK
    nch = (cnt_ref[s * N_EXPERTS + e] + (ch - 1)) // ch

    @pl.when((e == 0) & (f == 0))
    def _():
        o_ref[...] = jnp.zeros_like(o_ref)

    @pl.when(f == 0)
    def _():
        prow = post_ref[pl.ds(e, 1), :]
        slot = lax.broadcasted_iota(I32, (ch, tm), 0).astype(F32)

        def gather(c, carry):
            r0 = pl.multiple_of(c * ch, ch)
            sel = jnp.where(prow - (c * ch).astype(F32) == slot, 1.0, 0.0).astype(BF16)
            xc_ref[pl.ds(r0, ch), :] = _dot(sel, hn_ref[...]).astype(BF16)
            y_ref[pl.ds(r0, ch), :] = jnp.zeros((ch, y_ref.shape[1]), F32)
            return carry
        lax.fori_loop(0, nch + 1, gather, 0)

    def expert(c, carry):
        r0 = pl.multiple_of(c * ch, ch)
        xc = xc_ref[pl.ds(r0, ch), :]
        a = _silu(_dot(xc, w1_ref[...])) * _dot(xc, w3_ref[...])
        y_ref[pl.ds(r0, ch), :] += _dot(a.astype(BF16), w2_ref[...])
        return carry
    lax.fori_loop(0, nch, expert, 0)

    @pl.when(f == nf - 1)
    def _():
        lane = lax.broadcasted_iota(I32, (tm, LANES), 1)
        pcol = jnp.sum(jnp.where(lane == e, pos_ref[...], 0.0), axis=-1, keepdims=True)
        gcol = jnp.sum(jnp.where(lane == e, comb_ref[...], 0.0), axis=-1, keepdims=True)
        slot = lax.broadcasted_iota(I32, (tm, 2 * ch), 1).astype(F32)
        sub = 512

        def scatter(c2, carry):
            r0 = pl.multiple_of(c2 * (2 * ch), 2 * ch)
            z = y_ref[pl.ds(r0, 2 * ch), :].astype(BF16)
            sel = jnp.where(pcol - (c2 * (2 * ch)).astype(F32) == slot, 1.0, 0.0).astype(BF16)
            for t0 in range(0, tm, sub):
                o_ref[t0:t0 + sub, :] += gcol[t0:t0 + sub] * _dot(sel[t0:t0 + sub], z)
            return carry
        lax.fori_loop(0, (nch + 1) // 2, scatter, 0)


def _moe(hn, comb, pos, post, cnt, w1, w3, w2):
    n, d = hn.shape
    ne, _, dff = w1.shape
    ns, _, tm = post.shape
    tf = dff // 4
    cap = tm + 2 * MOE_CHUNK
    grid_spec = pltpu.PrefetchScalarGridSpec(
        num_scalar_prefetch=1,
        grid=(ns, ne, dff // tf),
        in_specs=[
            pl.BlockSpec((tm, d), lambda i, e, f, c: (i, 0)),
            pl.BlockSpec((tm, LANES), lambda i, e, f, c: (i, 0)),
            pl.BlockSpec((tm, LANES), lambda i, e, f, c: (i, 0)),
            pl.BlockSpec((None, N_EXPERTS, tm), lambda i, e, f, c: (i, 0, 0)),
            pl.BlockSpec((None, d, tf), lambda i, e, f, c: (e, 0, f)),
            pl.BlockSpec((None, d, tf), lambda i, e, f, c: (e, 0, f)),
            pl.BlockSpec((None, tf, d), lambda i, e, f, c: (e, f, 0)),
        ],
        out_specs=pl.BlockSpec((tm, d), lambda i, e, f, c: (i, 0)),
        scratch_shapes=[pltpu.VMEM((cap, d), BF16), pltpu.VMEM((cap, d), F32)],
    )
    return pl.pallas_call(
        _moe_kernel,
        grid_spec=grid_spec,
        out_shape=jax.ShapeDtypeStruct((n, d), F32),
        compiler_params=_cparams(("arbitrary", "arbitrary", "arbitrary")),
        name="ffn_experts",
    )(cnt, hn, comb, pos, post, w1, w3, w2)


def _residual_kernel(x_ref, f_ref, gate_ref, gn_ref, o_ref, *, final):
    y = x_ref[...] + gate_ref[...] * f_ref[...]
    o_ref[...] = _rms(y, gn_ref[...]) if final else y


def _residual(x2, f2, mod3, gn, s_len, final):
    n, d = x2.shape
    tm = TM_FFN
    tpb = s_len // tm
    row = pl.BlockSpec((tm, d), lambda i: (i, 0))
    return pl.pallas_call(
        functools.partial(_residual_kernel, final=final),
        grid=(n // tm,),
        in_specs=[row, row, pl.BlockSpec((None, 1, d), lambda i: (i // tpb, 0, 5)),
                  pl.BlockSpec((1, d), lambda i: (0, 0))],
        out_specs=row,
        out_shape=jax.ShapeDtypeStruct((n, d), F32),
        compiler_params=_cparams(("arbitrary",)),
        name="residual",
    )(x2, f2, mod3, gn)


def _final_kernel(x_ref, g_ref, o_ref):
    o_ref[...] = _rms(x_ref[...], g_ref[...])


def _final_norm(x2, g):
    n, d = x2.shape
    tm = TM_FFN
    return pl.pallas_call(
        _final_kernel,
        grid=(n // tm,),
        in_specs=[pl.BlockSpec((tm, d), lambda i: (i, 0)), pl.BlockSpec((1, d), lambda i: (0, 0))],
        out_specs=pl.BlockSpec((tm, d), lambda i: (i, 0)),
        out_shape=jax.ShapeDtypeStruct((n, d), F32),
        compiler_params=_cparams(("arbitrary",)),
        name="final_norm",
    )(x2, g)


def _alibi_slopes():
    return 2.0 ** (-8.0 * jnp.arange(1, N_HEADS + 1, dtype=F32) / N_HEADS)


def _block_tables():
    j = jnp.arange(TQ)[:, None]
    i = jnp.arange(TQ)[None, :]
    diagm = jnp.where((j // CHUNK) <= (i // CHUNK), 0.0, NEG).astype(F32)
    corr = -2.0 * _alibi_slopes()[:, None, None] * jnp.maximum(j - i, 0).astype(F32)[None]
    return corr, diagm


def _rope_tables(s_len):
    inv = ROPE_THETA ** (-jnp.arange(0, QK_ROPE, 2, dtype=F32) / QK_ROPE)
    ang = jnp.arange(s_len, dtype=F32)[:, None] * inv[None, :]
    cos2 = jnp.concatenate([jnp.cos(ang)] * 2, axis=-1)
    sin2 = jnp.concatenate([jnp.sin(ang)] * 2, axis=-1)
    z = jnp.zeros((s_len, QK_NOPE), F32)
    z2 = jnp.zeros((s_len, LANES - QK_NOPE - QK_ROPE), F32)
    return (jnp.concatenate([z, cos2, z2], axis=-1), jnp.concatenate([z, sin2, z2], axis=-1))


def _rot_cols(w):
    half = QK_ROPE // 2
    return jnp.concatenate([-w[..., half:], w[..., :half]], axis=-1)


def _slabs(a, nh, extra=None):
    b, s_len, _ = a.shape
    parts = [a.reshape(b, s_len, nh, -1)]
    if extra is not None:
        parts.append(jnp.broadcast_to(extra, (b, s_len, nh, extra.shape[-1])).astype(a.dtype))
    used = sum(p.shape[-1] for p in parts)
    parts.append(jnp.zeros((b, s_len, nh, LANES - used), a.dtype))
    return jnp.concatenate(parts, axis=-1)


def _rows(slab):
    b, s_len = slab.shape[:2]
    return slab.reshape(b, s_len, -1)


def _cols(slab):
    b, s_len = slab.shape[:2]
    return jnp.transpose(slab, (0, 2, 3, 1)).reshape(b, -1, s_len)


def _alibi_extras(s_len, nslab):
    loc = (jnp.arange(s_len) % TQ).astype(F32)[:, None, None]
    sl = jnp.repeat(_alibi_slopes(), nslab // N_HEADS)[None, :, None]
    q_extra = jnp.concatenate([-sl * loc, jnp.broadcast_to(sl, (s_len, nslab, 1))], axis=-1)
    k_extra = jnp.concatenate([jnp.ones((s_len, nslab, 1), F32),
                               jnp.broadcast_to(loc, (s_len, nslab, 1))], axis=-1)
    return q_extra[None], k_extra[None]


def _layer_weights(l, w_in, mla_w_uq, mla_w_ukv):
    w = w_in[l]
    names = ('a_q', 'a_k', 'a_v', 'b_qd', 'b_kvd', 'b_kr', 'c_q', 'c_k', 'c_v', 'c_qi', 'c_ki',
             'c_wi', 'd_q', 'd_k', 'd_v')
    widths = (256, 256, 256, Q_LORA, KV_LORA, QK_ROPE, 256, 256, 256, IDX_HEADS * IDX_DIM, IDX_DIM,
              IDX_HEADS, 256, 256, 256)
    cols, o = {}, 0
    for nme, wd in zip(names, widths):
        cols[nme] = w[:, o:o + wd]
        o += wd
    pad = jnp.zeros((w.shape[0], N_SIDE - (Q_LORA + KV_LORA + 2 * QK_ROPE + IDX_DIM + IDX_HEADS)), F32)
    wcat = jnp.concatenate(
        [cols['a_q'], cols['a_k'], cols['a_v'], cols['c_q'], cols['c_k'], cols['c_v'], cols['c_qi'],
         cols['d_q'], cols['d_k'], cols['d_v'],
         cols['b_qd'], cols['b_kvd'], cols['b_kr'], _rot_cols(cols['b_kr']), cols['c_ki'],
         cols['c_wi'], pad], axis=-1).astype(BF16)
    ones = lambda k: jnp.ones((k,), F32)
    cs = jnp.concatenate([
        ones(256) * DIFF_D ** -0.5, ones(512),
        ones(256) * HEAD_DIM ** -0.5, ones(768),
        ones(256) * HEAD_DIM ** -0.5, ones(512), ones(N_SIDE)])[None, :]

    uq = mla_w_uq[l].reshape(Q_LORA, N_HEADS, QK_NOPE + QK_ROPE)
    zq = jnp.zeros((Q_LORA, N_HEADS, LANES - QK_NOPE - QK_ROPE), F32)
    wqa = jnp.concatenate([uq, zq], axis=-1).reshape(Q_LORA, N_HEADS * LANES).astype(BF16)
    wqb = jnp.concatenate([jnp.zeros((Q_LORA, N_HEADS, QK_NOPE), F32), _rot_cols(uq[..., QK_NOPE:]), zq],
                          axis=-1).reshape(Q_LORA, N_HEADS * LANES).astype(BF16)
    ukv = mla_w_ukv[l].reshape(KV_LORA, N_HEADS, QK_NOPE + HEAD_DIM)
    zk = jnp.zeros((KV_LORA, N_HEADS, LANES - QK_NOPE), F32)
    wk = jnp.concatenate([ukv[..., :QK_NOPE], zk], axis=-1).reshape(KV_LORA, N_HEADS * LANES).astype(BF16)
    wv = jnp.concatenate([ukv[..., QK_NOPE:], zk], axis=-1).reshape(KV_LORA, N_HEADS * LANES).astype(BF16)
    return wcat, cs, wqa, wqb, wk, wv


def _placement():
    r = jnp.arange(LANES)[:, None]
    c = jnp.arange(N_HEADS * LANES)[None, :] % LANES
    e1 = ((c >= QK_NOPE) & (c < QK_NOPE + QK_ROPE) & (r == c - QK_NOPE)).astype(BF16)
    e2 = ((c >= QK_NOPE) & (c < QK_NOPE + QK_ROPE) & (r == c - QK_NOPE + QK_ROPE)).astype(BF16)
    return e1, e2


def kernel(x, c, ada_w, ada_b, mix_norm_g, ffn_norm_g, w_in, w_out, diff_lambda, diff_norm_g, mla_q_norm_g, mla_kv_norm_g, mla_w_uq, mla_w_ukv, band_rel_bias, ffn_w1, ffn_w3, ffn_w2, moe_router, moe_w1, moe_w3, moe_w2, final_norm_g):
    b, s_len, d = x.shape
    depth = ada_w.shape[0]
    n = b * s_len
    n_sel = min(TOPK_MAX, s_len // 4)
    assert d == D_MODEL and s_len % TM_FFN == 0 and s_len % TQ == 0

    mod = _modulation(c, ada_w, ada_b)
    corr, diagm = _block_tables()
    slopes = _alibi_slopes()
    cos_t, sin_t = _rope_tables(s_len)
    e1, e2 = _placement()
    tril = (jnp.arange(TQ)[None, :] <= jnp.arange(TQ)[:, None]).astype(BF16)
    mla_scale = (QK_NOPE + QK_ROPE) ** -0.5
    qx8, kx8 = _alibi_extras(s_len, 2 * N_HEADS)
    qx4, kx4 = _alibi_extras(s_len, N_HEADS)
    one4 = jnp.ones((1, s_len, N_HEADS, 1), F32)
    tr = lambda a: jnp.swapaxes(a, 1, 2)
    front = 2 * TQ

    x2 = x.reshape(n, d)
    for l in range(depth):
        mod3 = mod[l].reshape(b, 1, 6 * d)
        wcat, cs, wqa, wqb, wk, wv = _layer_weights(l, w_in, mla_w_uq, mla_w_ukv)
        main, side = _inproj(x2, mix_norm_g[l][None], mod3, wcat, cs, s_len)
        main3 = main.reshape(b, s_len, N_MAIN)
        side3 = side.reshape(b, s_len, N_SIDE)
        blk = lambda k: main3[:, :, k * 256:(k + 1) * 256]

        lam_init = 0.8 - 0.6 * math.exp(-0.3 * l)
        o_a = _attn_a(_cols(_slabs(blk(0), 2 * N_HEADS, qx8)), _rows(_slabs(blk(1), 2 * N_HEADS, kx8)),
                      _cols(_slabs(blk(2), N_HEADS, one4)), corr + diagm[None], slopes, diff_lambda[l],
                      jnp.broadcast_to(diff_norm_g[l][:, None], (2 * DIFF_D, TQ)), lam_init)

        qb_, kb_, vb_ = _mla_prep(side, mla_q_norm_g[l][None], mla_kv_norm_g[l][None], wqa, wqb, wk, wv,
                                  e1, e2, cos_t, sin_t, s_len, mla_scale)
        o_b = _attn_b(tr(qb_.reshape(b, s_len, -1)), kb_.reshape(b, s_len, -1),
                      tr(vb_.reshape(b, s_len, -1)), diagm)

        o0 = Q_LORA + KV_LORA + 2 * QK_ROPE
        ki = side3[:, :, o0:o0 + IDX_DIM].astype(BF16)
        wt = tr(side3[:, :, o0 + IDX_DIM:o0 + IDX_DIM + IDX_HEADS])
        o_c = _attn_c(_cols(_slabs(blk(3), N_HEADS, qx4)), tr(blk(6)), wt, _rows(_slabs(blk(4), N_HEADS, kx4)),
                      ki, _cols(_slabs(blk(5), N_HEADS, one4)), corr, diagm, slopes, tril, n_sel)

        k_pad = jnp.pad(_rows(_slabs(blk(8), N_HEADS)), ((0, 0), (front, 0), (0, 0)))
        vt_pad = jnp.pad(_cols(_slabs(blk(9), N_HEADS, one4)), ((0, 0), (0, 0), (front, 0)))
        o_d = _attn_d(_cols(_slabs(blk(7), N_HEADS)), k_pad, vt_pad, _band_bias(band_rel_bias[l]))

        flat = lambda a: a.reshape(n, 256)
        x2 = _outproj(x2, flat(o_a), flat(o_b), flat(o_c), flat(o_d), w_out[l].astype(BF16), mod3, s_len)

        gf = ffn_norm_g[l][None]
        if l % 2 == 0:
            i = l // 2
            x2 = _ffn(x2, gf, mod3, ffn_w1[i].astype(BF16), ffn_w3[i].astype(BF16),
                      ffn_w2[i].astype(BF16), s_len)
        else:
            i = l // 2
            r = jnp.pad(moe_router[i], ((0, 0), (0, LANES - N_EXPERTS)))
            rhi = r.astype(BF16)
            rlo = (r - rhi.astype(F32)).astype(BF16)
            tril512 = (jnp.arange(512)[None, :] <= jnp.arange(512)[:, None]).astype(BF16)
            hn, comb, pos, post, cnt = _router(x2, gf, mod3, rhi, rlo, tril512, s_len)
            cnt_i = cnt[:, 0, :N_EXPERTS].astype(I32).reshape(-1)
            f2 = _moe(hn, comb, pos, post, cnt_i, moe_w1[i].astype(BF16), moe_w3[i].astype(BF16),
                      moe_w2[i].astype(BF16))
            last = l == depth - 1
            x2 = _residual(x2, f2, mod3, final_norm_g[None], s_len, last)
            if last:
                return x2.reshape(b, s_len, d)
    return _final_norm(x2, final_norm_g[None]).reshape(b, s_len, d)
```

```python
import functools
import math

import numpy as np
import jax
import jax.numpy as jnp
from jax import lax
from jax.experimental import pallas as pl
from jax.experimental.pallas import tpu as pltpu

F32 = jnp.float32
BF16 = jnp.bfloat16
I32 = jnp.int32

D_MODEL = 1024
CHUNK = 64
N_HEADS = 4
HEAD_DIM = 64
DIFF_D = 32
Q_LORA = 256
KV_LORA = 128
QK_NOPE = 64
QK_ROPE = 32
ROPE_THETA = 10000.0
IDX_HEADS = 8
IDX_DIM = 32
TOPK_MAX = 256
BAND_CHUNKS = 8
REL_CLIP = 128
N_EXPERTS = 8
EPS = 1e-6

NEG = -1e30
INT_MIN = -(2 ** 31)

LANES = 128
VMEM_LIMIT = 56 * 1024 * 1024
TQ = 256
SWEEP_UNITS = 16
TM_PROJ = 512
TM_FFN = 512
TM_MOE = 2048
MOE_CHUNK = 128
MOE_SUB = 512
N_MAIN = 2560
N_SIDE = 512


def _cparams(sem):
    return pltpu.CompilerParams(dimension_semantics=sem, vmem_limit_bytes=VMEM_LIMIT)


def _dot(a, b):
    return jnp.dot(a, b, preferred_element_type=F32)


def _silu(x):
    return x / (1.0 + jnp.exp(-x))


def _rms(x, g):
    return x * lax.rsqrt(jnp.mean(x * x, axis=-1, keepdims=True) + EPS) * g


def _adaln(x, g, shift, scale):
    return _rms(x, g) * (1.0 + scale) + shift


def _mod_kernel(c_ref, w_ref, b_ref, o_ref):
    sc = _silu(c_ref[...]).astype(BF16)
    o_ref[...] = _dot(sc, w_ref[...].astype(BF16)) + b_ref[...]


def _modulation(c, ada_w, ada_b):
    depth, d, n6 = ada_w.shape
    b = c.shape[0]
    tn = 1536
    return pl.pallas_call(
        _mod_kernel,
        grid=(depth, n6 // tn),
        in_specs=[
            pl.BlockSpec((b, d), lambda l, j: (0, 0)),
            pl.BlockSpec((None, d, tn), lambda l, j: (l, 0, j)),
            pl.BlockSpec((None, 1, tn), lambda l, j: (l, 0, j)),
        ],
        out_specs=pl.BlockSpec((None, b, tn), lambda l, j: (l, 0, j)),
        out_shape=jax.ShapeDtypeStruct((depth, b, n6), F32),
        compiler_params=_cparams(("arbitrary", "arbitrary")),
        name="modulation",
    )(c, ada_w, ada_b.reshape(depth, 1, n6))


def _inproj_kernel(x_ref, g_ref, sh_ref, sc_ref, w_ref, cs_ref, main_ref, side_ref):
    hn = _adaln(x_ref[...], g_ref[...], sh_ref[...], sc_ref[...]).astype(BF16)
    step = 512
    for c0 in range(0, N_MAIN + N_SIDE, step):
        acc = _dot(hn, w_ref[:, c0:c0 + step]) * cs_ref[:, c0:c0 + step]
        if c0 < N_MAIN:
            main_ref[:, c0:c0 + step] = acc.astype(BF16)
        else:
            side_ref[:, c0 - N_MAIN:c0 - N_MAIN + step] = acc


def _inproj(x2, g, mod3, w, cs, s_len):
    n, d = x2.shape
    tm = TM_PROJ
    tpb = s_len // tm
    nc = N_MAIN + N_SIDE
    return pl.pallas_call(
        _inproj_kernel,
        grid=(n // tm,),
        in_specs=[
            pl.BlockSpec((tm, d), lambda i: (i, 0)),
            pl.BlockSpec((1, d), lambda i: (0, 0)),
            pl.BlockSpec((None, 1, d), lambda i: (i // tpb, 0, 0)),
            pl.BlockSpec((None, 1, d), lambda i: (i // tpb, 0, 1)),
            pl.BlockSpec((d, nc), lambda i: (0, 0)),
            pl.BlockSpec((1, nc), lambda i: (0, 0)),
        ],
        out_specs=[
            pl.BlockSpec((tm, N_MAIN), lambda i: (i, 0)),
            pl.BlockSpec((tm, N_SIDE), lambda i: (i, 0)),
        ],
        out_shape=[
            jax.ShapeDtypeStruct((n, N_MAIN), BF16),
            jax.ShapeDtypeStruct((n, N_SIDE), F32),
        ],
        compiler_params=_cparams(("arbitrary",)),
        name="inproj",
    )(x2, g, mod3, mod3, w, cs)


def _softmax_update(s, c, m, acc_ref, slot, vt):
    m_new = jnp.maximum(m, jnp.max(s, axis=0, keepdims=True) + c)
    alpha = jnp.exp(m - m_new)
    p = jnp.exp(s - (m_new - c))
    acc_ref[slot] = acc_ref[slot] * alpha + _dot(vt, p.astype(BF16))
    return m_new


def _normalized(acc_ref, slot):
    a = acc_ref[slot]
    return a[:HEAD_DIM] / a[HEAD_DIM:HEAD_DIM + 1]


def _pipelined(n, produce, consume, lookahead=3):
    vals = [produce(i) for i in range(min(lookahead, n))]
    for i in range(n):
        if i + lookahead < n:
            vals.append(produce(i + lookahead))
        consume(i, vals[i])
        vals[i] = None


def _sweep(qb, body, carry, group):
    ng = qb // group
    carry = lax.fori_loop(0, ng, lambda g, c: body(g * group, c, group), carry)
    return lax.fori_loop(ng * group, qb, lambda kb, c: body(kb, c, 1), carry)


def _block_const(slope_ref, h, qb, kb, tq):
    return -slope_ref[h] * ((qb - kb) * tq).astype(F32)


def _attn_a_kernel(qt_ref, k_ref, vt_ref, corr_ref, slope_ref, lamv_ref, g_ref, o_ref, acc_ref,
                   *, lam_init):
    qb = pl.program_id(1)
    tq = o_ref.shape[0]
    nhm = 2 * N_HEADS
    acc_ref[...] = jnp.zeros_like(acc_ref)

    def block(kb, ms, nb, diag=False):
        out = list(ms)
        start = lambda u: pl.multiple_of((kb + u // nhm) * tq, tq)

        def logits(u):
            i = u % nhm
            return _dot(k_ref[pl.ds(start(u), tq), i * LANES:(i + 1) * LANES], qt_ref[i * LANES:(i + 1) * LANES, :])

        def update(u, s):
            i = u % nhm
            h = i // 2
            vt = vt_ref[h * LANES:(h + 1) * LANES, pl.ds(start(u), tq)]
            if diag:
                s, c = corr_ref[h] + s, 0.0
            else:
                c = _block_const(slope_ref, h, qb, kb + u // nhm, tq)
            out[i] = _softmax_update(s, c, out[i], acc_ref, i, vt)

        _pipelined(nb * nhm, logits, update)
        return tuple(out)

    ms = tuple(jnp.full((1, tq), NEG, F32) for _ in range(nhm))
    ms = block(qb, ms, 1, diag=True)
    _sweep(qb, block, ms, SWEEP_UNITS // nhm)

    lv = lamv_ref[...]
    lam = (jnp.exp(jnp.sum(lv[0:1] * lv[1:2], axis=-1, keepdims=True))
           - jnp.exp(jnp.sum(lv[2:3] * lv[3:4], axis=-1, keepdims=True)) + lam_init)
    outs = []
    for h in range(N_HEADS):
        o = _normalized(acc_ref, 2 * h) - lam * _normalized(acc_ref, 2 * h + 1)
        o = o * lax.rsqrt(jnp.mean(o * o, axis=0, keepdims=True) + EPS) * g_ref[...]
        outs.append(o * (1.0 - lam_init))
    o_ref[...] = jnp.transpose(jnp.concatenate(outs, axis=0)).astype(o_ref.dtype)


def _attn_a(qt, k, vt, corr, slopes, lamv, gcol, lam_init):
    b, s_len, _ = k.shape
    nq = s_len // TQ
    return pl.pallas_call(
        functools.partial(_attn_a_kernel, lam_init=lam_init),
        grid=(b, nq),
        in_specs=[
            pl.BlockSpec((None, 2 * N_HEADS * LANES, TQ), lambda i, j: (i, 0, j)),
            pl.BlockSpec((None, s_len, 2 * N_HEADS * LANES), lambda i, j: (i, 0, 0)),
            pl.BlockSpec((None, N_HEADS * LANES, s_len), lambda i, j: (i, 0, 0)),
            pl.BlockSpec((N_HEADS, TQ, TQ), lambda i, j: (0, 0, 0)),
            pl.BlockSpec(memory_space=pltpu.SMEM),
            pl.BlockSpec((4, DIFF_D), lambda i, j: (0, 0)),
            pl.BlockSpec((2 * DIFF_D, TQ), lambda i, j: (0, 0)),
        ],
        out_specs=pl.BlockSpec((None, TQ, 256), lambda i, j: (i, j, 0)),
        out_shape=jax.ShapeDtypeStruct((b, s_len, 256), BF16),
        scratch_shapes=[pltpu.VMEM((2 * N_HEADS, LANES, TQ), F32)],
        compiler_params=_cparams(("arbitrary", "arbitrary")),
        name="attn_diff",
    )(qt, k, vt, corr, slopes, lamv, gcol)


def _mla_prep_kernel(side_ref, gq_ref, gkv_ref, wqa_ref, wqb_ref, wk_ref, wv_ref, e1_ref, e2_ref,
                     cos_ref, sin_ref, q_ref, k_ref, v_ref, *, qscale):
    side = side_ref[...]
    hq = _rms(side[:, :Q_LORA], gq_ref[...]).astype(BF16)
    hkv = _rms(side[:, Q_LORA:Q_LORA + KV_LORA], gkv_ref[...]).astype(BF16)
    misc = side[:, Q_LORA + KV_LORA:]
    cos4 = jnp.concatenate([cos_ref[...]] * N_HEADS, axis=-1)
    sin4 = jnp.concatenate([sin_ref[...]] * N_HEADS, axis=-1)
    lane = lax.broadcasted_iota(I32, (1, N_HEADS * LANES), 1) % LANES
    nope = (lane < QK_NOPE).astype(F32)
    ones_col = (lane == HEAD_DIM).astype(F32)
    q = (_dot(hq, wqa_ref[...]) * (nope + cos4) + _dot(hq, wqb_ref[...]) * sin4) * qscale
    q_ref[...] = q.astype(BF16)
    hi = misc.astype(BF16)
    lo = (misc - hi.astype(F32)).astype(BF16)
    kr = _dot(hi, e1_ref[...]) + _dot(lo, e1_ref[...])
    krp = _dot(hi, e2_ref[...]) + _dot(lo, e2_ref[...])
    k = _dot(hkv, wk_ref[...]) + kr * cos4 + krp * sin4
    k_ref[...] = k.astype(BF16)
    v_ref[...] = (_dot(hkv, wv_ref[...]) + ones_col).astype(BF16)


def _mla_prep(side, gq, gkv, wqa, wqb, wk, wv, e1, e2, cos_t, sin_t, s_len, qscale):
    n = side.shape[0]
    tm = TM_PROJ
    tpb = s_len // tm
    full = lambda a: pl.BlockSpec(a.shape, lambda i: (0,) * a.ndim)
    out = jax.ShapeDtypeStruct((n, N_HEADS * LANES), BF16)
    return pl.pallas_call(
        functools.partial(_mla_prep_kernel, qscale=qscale),
        grid=(n // tm,),
        in_specs=[pl.BlockSpec((tm, N_SIDE), lambda i: (i, 0)),
                  full(gq), full(gkv), full(wqa), full(wqb), full(wk), full(wv), full(e1), full(e2),
                  pl.BlockSpec((tm, LANES), lambda i: (i % tpb, 0)),
                  pl.BlockSpec((tm, LANES), lambda i: (i % tpb, 0))],
        out_specs=[pl.BlockSpec((tm, N_HEADS * LANES), lambda i: (i, 0))] * 3,
        out_shape=[out, out, out],
        compiler_params=_cparams(("arbitrary",)),
        name="mla_prep",
    )(side, gq, gkv, wqa, wqb, wk, wv, e1, e2, cos_t, sin_t)


def _attn_b_kernel(qt_ref, k_ref, vt_ref, diagm_ref, o_ref, acc_ref):
    qb = pl.program_id(1)
    tq = o_ref.shape[0]
    acc_ref[...] = jnp.zeros_like(acc_ref)

    def block(kb, ms, nb, diag=False):
        out = list(ms)
        start = lambda u: pl.multiple_of((kb + u // N_HEADS) * tq, tq)

        def logits(u):
            h = u % N_HEADS
            return _dot(k_ref[pl.ds(start(u), tq), h * LANES:(h + 1) * LANES], qt_ref[h * LANES:(h + 1) * LANES, :])

        def update(u, s):
            h = u % N_HEADS
            vt = vt_ref[h * LANES:(h + 1) * LANES, pl.ds(start(u), tq)]
            if diag:
                s = diagm_ref[...] + s
            out[h] = _softmax_update(s, 0.0, out[h], acc_ref, h, vt)

        _pipelined(nb * N_HEADS, logits, update)
        return tuple(out)

    ms = tuple(jnp.full((1, tq), NEG, F32) for _ in range(N_HEADS))
    ms = block(qb, ms, 1, diag=True)
    _sweep(qb, block, ms, SWEEP_UNITS // N_HEADS)
    o_ref[...] = jnp.transpose(jnp.concatenate([_normalized(acc_ref, h) for h in range(N_HEADS)],
                                               axis=0)).astype(o_ref.dtype)


def _attn_b(qt, k, vt, diagm):
    b, s_len, _ = k.shape
    nq = s_len // TQ
    return pl.pallas_call(
        _attn_b_kernel,
        grid=(b, nq),
        in_specs=[
            pl.BlockSpec((None, N_HEADS * LANES, TQ), lambda i, j: (i, 0, j)),
            pl.BlockSpec((None, s_len, N_HEADS * LANES), lambda i, j: (i, 0, 0)),
            pl.BlockSpec((None, N_HEADS * LANES, s_len), lambda i, j: (i, 0, 0)),
            pl.BlockSpec((TQ, TQ), lambda i, j: (0, 0)),
        ],
        out_specs=pl.BlockSpec((None, TQ, 256), lambda i, j: (i, j, 0)),
        out_shape=jax.ShapeDtypeStruct((b, s_len, 256), BF16),
        scratch_shapes=[pltpu.VMEM((N_HEADS, LANES, TQ), F32)],
        compiler_params=_cparams(("arbitrary", "arbitrary")),
        name="attn_latent",
    )(qt, k, vt, diagm)


def _attn_c_kernel(qt_ref, qit_ref, wt_ref, k_ref, ki_ref, vt_ref, corr_ref, diagm_ref, slope_ref,
                   tril_ref, o_ref, keys_ref, acc_ref, *, n_sel):
    qb = pl.program_id(1)
    tq = o_ref.shape[0]
    nkb = qb + 1

    wt = wt_ref[...] * (IDX_HEADS ** -0.5 * IDX_DIM ** -0.5)

    def score_block(kb, diag):
        k0 = pl.multiple_of(kb * tq, tq)
        ki = ki_ref[pl.ds(k0, tq), :]
        total = [jnp.zeros((tq, tq), F32)]

        def weighted(h, d):
            total[0] = total[0] + jnp.maximum(d, 0.0) * wt[h:h + 1]

        _pipelined(IDX_HEADS, lambda h: _dot(ki, qit_ref[h * IDX_DIM:(h + 1) * IDX_DIM, :]), weighted)
        sc = jnp.where(total[0] == 0.0, 0.0, total[0])
        bits = lax.bitcast_convert_type(sc, I32)
        key = bits ^ ((bits >> 31) & 0x7FFFFFFF)
        if diag:
            key = jnp.where(diagm_ref[...] < 0.0, INT_MIN, key)
        keys_ref[pl.ds(k0, tq), :] = key

    score_block(qb, True)

    def _score_loop(kb, c):
        score_block(kb, False)
        return c
    lax.fori_loop(0, qb, _score_loop, 0)

    def count_ge(cand):
        def body(kb, cnt8):
            k0 = pl.multiple_of(kb * tq, tq)
            hit = jnp.where(keys_ref[pl.ds(k0, tq), :] >= cand, 1, 0)
            return cnt8 + jnp.sum(hit.reshape(tq // 8, 8, tq), axis=0)
        cnt8 = lax.fori_loop(0, nkb, body, jnp.zeros((8, tq), I32))
        return jnp.sum(cnt8, axis=0, keepdims=True)

    zero = jnp.zeros((1, tq), I32)
    thr = jnp.where(count_ge(zero) >= n_sel, zero, INT_MIN)

    def bit_body(i, thr):
        cand = thr + (jnp.int32(1) << (30 - i))
        return jnp.where(count_ge(cand) >= n_sel, cand, thr)
    thr = lax.fori_loop(0, 31, bit_body, thr)
    thr = jnp.maximum(thr, INT_MIN + 1)
    c_gt = count_ge(thr + 1)
    c_ge = count_ge(thr)
    need = (n_sel - c_gt).astype(F32)
    has_ties = jnp.max(c_ge) > n_sel

    def attend(kb, carry, nb, diag=False, ties=False):
        ms, eq_before = carry
        out = list(ms)
        start = lambda j: pl.multiple_of((kb + j) * tq, tq)
        sels = []
        for j in range(nb):
            key = keys_ref[pl.ds(start(j), tq), :]
            if ties:
                eq = jnp.where(key == thr, 1.0, 0.0)
                rank = _dot(tril_ref[...], eq.astype(BF16)) + eq_before
                sels.append((key > thr) | ((key == thr) & (rank <= need)))
                eq_before = eq_before + jnp.sum(eq, axis=0, keepdims=True)
            else:
                sels.append(key >= thr)

        def logits(u):
            h, j = u % N_HEADS, u // N_HEADS
            return _dot(k_ref[pl.ds(start(j), tq), h * LANES:(h + 1) * LANES], qt_ref[h * LANES:(h + 1) * LANES, :])

        def update(u, s):
            h, j = u % N_HEADS, u // N_HEADS
            vt = vt_ref[h * LANES:(h + 1) * LANES, pl.ds(start(j), tq)]
            if diag:
                s, c = corr_ref[h] + s, 0.0
            else:
                c = _block_const(slope_ref, h, qb, kb + j, tq)
            s = jnp.where(sels[j], s, NEG)
            out[h] = _softmax_update(s, c, out[h], acc_ref, h, vt)

        _pipelined(nb * N_HEADS, logits, update)
        return tuple(out), eq_before

    def run(ties):
        acc_ref[...] = jnp.zeros_like(acc_ref)
        carry = (tuple(jnp.full((1, tq), NEG, F32) for _ in range(N_HEADS)), jnp.zeros((1, tq), F32))
        carry = _sweep(qb, lambda kb, c, nb: attend(kb, c, nb, ties=ties), carry, SWEEP_UNITS // N_HEADS)
        attend(qb, carry, 1, diag=True, ties=ties)

    @pl.when(has_ties)
    def _():
        run(True)

    @pl.when(jnp.logical_not(has_ties))
    def _():
        run(False)

    o_ref[...] = jnp.transpose(jnp.concatenate([_normalized(acc_ref, h) for h in range(N_HEADS)],
                                               axis=0)).astype(o_ref.dtype)


def _attn_c(qt, qit, wt, k, ki, vt, corr, diagm, slopes, tril, n_sel):
    b, s_len, _ = k.shape
    nq = s_len // TQ
    return pl.pallas_call(
        functools.partial(_attn_c_kernel, n_sel=n_sel),
        grid=(b, nq),
        in_specs=[
            pl.BlockSpec((None, N_HEADS * LANES, TQ), lambda i, j: (i, 0, j)),
            pl.BlockSpec((None, IDX_HEADS * IDX_DIM, TQ), lambda i, j: (i, 0, j)),
            pl.BlockSpec((None, IDX_HEADS, TQ), lambda i, j: (i, 0, j)),
            pl.BlockSpec((None, s_len, N_HEADS * LANES), lambda i, j: (i, 0, 0)),
            pl.BlockSpec((None, s_len, IDX_DIM), lambda i, j: (i, 0, 0)),
            pl.BlockSpec((None, N_HEADS * LANES, s_len), lambda i, j: (i, 0, 0)),
            pl.BlockSpec((N_HEADS, TQ, TQ), lambda i, j: (0, 0, 0)),
            pl.BlockSpec((TQ, TQ), lambda i, j: (0, 0)),
            pl.BlockSpec(memory_space=pltpu.SMEM),
            pl.BlockSpec((TQ, TQ), lambda i, j: (0, 0)),
        ],
        out_specs=pl.BlockSpec((None, TQ, 256), lambda i, j: (i, j, 0)),
        out_shape=jax.ShapeDtypeStruct((b, s_len, 256), BF16),
        scratch_shapes=[pltpu.VMEM((s_len, TQ), I32),
                        pltpu.VMEM((N_HEADS, LANES, TQ), F32)],
        compiler_params=_cparams(("arbitrary", "arbitrary")),
        name="attn_sparse",
    )(qt, qit, wt, k, ki, vt, corr, diagm, slopes, tril)


def _band_bias_kernel(rb_ref, o_ref):
    h = pl.program_id(0)
    nk, tq = o_ref.shape
    j = lax.broadcasted_iota(I32, (nk, tq), 0)
    t = lax.broadcasted_iota(I32, (nk, tq), 1) + (nk - tq)
    idx = jnp.clip(t - j, -REL_CLIP, REL_CLIP) + REL_CLIP
    cq, ck = t // CHUNK, j // CHUNK
    valid = (ck <= cq) & (ck >= cq - BAND_CHUNKS)
    tbl = lax.fori_loop(0, 2 * REL_CLIP + 1,
                        lambda r, tb: jnp.where(idx == r, rb_ref[h, r], tb), jnp.zeros((nk, tq), F32))
    o_ref[...] = jnp.where(valid, tbl, NEG)


def _band_bias(rel_bias):
    nh = rel_bias.shape[0]
    return pl.pallas_call(
        _band_bias_kernel,
        grid=(nh,),
        in_specs=[pl.BlockSpec(memory_space=pltpu.SMEM)],
        out_specs=pl.BlockSpec((None, 3 * TQ, TQ), lambda h: (h, 0, 0)),
        out_shape=jax.ShapeDtypeStruct((nh, 3 * TQ, TQ), F32),
        compiler_params=_cparams(("arbitrary",)),
        name="band_bias",
    )(rel_bias)


def _attn_d_kernel(qt_ref, k_ref, vt_ref, bias_ref, o_ref):
    qb = pl.program_id(1)
    tq = o_ref.shape[0]
    outs = [None] * N_HEADS

    def logits(h):
        ds = []
        for j in range(3):
            k0 = pl.multiple_of((qb + j) * tq, tq)
            ds.append(_dot(k_ref[pl.ds(k0, tq), h * LANES:(h + 1) * LANES], qt_ref[h * LANES:(h + 1) * LANES, :]))
        return ds

    def attend(h, ds):
        ss = []
        for j in range(3):
            s = bias_ref[h, j * tq:(j + 1) * tq, :] + ds[j]
            ss.append(jnp.where(qb + j >= 2, s, NEG))
        m = jnp.maximum(jnp.maximum(jnp.max(ss[0], axis=0, keepdims=True),
                                    jnp.max(ss[1], axis=0, keepdims=True)),
                        jnp.max(ss[2], axis=0, keepdims=True))
        acc = jnp.zeros((LANES, tq), F32)
        for j in range(3):
            k0 = pl.multiple_of((qb + j) * tq, tq)
            vt = vt_ref[h * LANES:(h + 1) * LANES, pl.ds(k0, tq)]
            acc = acc + _dot(vt, jnp.exp(ss[j] - m).astype(BF16))
        outs[h] = acc[:HEAD_DIM] / acc[HEAD_DIM:HEAD_DIM + 1]

    _pipelined(N_HEADS, logits, attend, lookahead=1)
    o_ref[...] = jnp.transpose(jnp.concatenate(outs, axis=0)).astype(o_ref.dtype)


def _attn_d(qt, k_pad, vt_pad, bias):
    b, sp, _ = k_pad.shape
    s_len = sp - 2 * TQ
    nq = s_len // TQ
    return pl.pallas_call(
        _attn_d_kernel,
        grid=(b, nq),
        in_specs=[
            pl.BlockSpec((None, N_HEADS * LANES, TQ), lambda i, j: (i, 0, j)),
            pl.BlockSpec((None, sp, N_HEADS * LANES), lambda i, j: (i, 0, 0)),
            pl.BlockSpec((None, N_HEADS * LANES, sp), lambda i, j: (i, 0, 0)),
            pl.BlockSpec((N_HEADS, 3 * TQ, TQ), lambda i, j: (0, 0, 0)),
        ],
        out_specs=pl.BlockSpec((None, TQ, 256), lambda i, j: (i, j, 0)),
        out_shape=jax.ShapeDtypeStruct((b, s_len, 256), BF16),
        compiler_params=_cparams(("arbitrary", "arbitrary")),
        name="attn_band",
    )(qt, k_pad, vt_pad, bias)


def _outproj_kernel(x_ref, oa_ref, ob_ref, oc_ref, od_ref, w_ref, gate_ref, o_ref):
    y = _dot(oa_ref[...], w_ref[0:256, :])
    y = y + _dot(ob_ref[...], w_ref[256:512, :])
    y = y + _dot(oc_ref[...], w_ref[512:768, :])
    y = y + _dot(od_ref[...], w_ref[768:1024, :])
    o_ref[...] = x_ref[...] + gate_ref[...] * y


def _outproj(x2, oa, ob, oc, od, w, mod3, s_len):
    n, d = x2.shape
    tm = TM_PROJ
    tpb = s_len // tm
    ospec = pl.BlockSpec((tm, 256), lambda i: (i, 0))
    return pl.pallas_call(
        _outproj_kernel,
        grid=(n // tm,),
        in_specs=[pl.BlockSpec((tm, d), lambda i: (i, 0)), ospec, ospec, ospec, ospec,
                  pl.BlockSpec((d, d), lambda i: (0, 0)),
                  pl.BlockSpec((None, 1, d), lambda i: (i // tpb, 0, 2))],
        out_specs=pl.BlockSpec((tm, d), lambda i: (i, 0)),
        out_shape=jax.ShapeDtypeStruct((n, d), F32),
        compiler_params=_cparams(("arbitrary",)),
        name="outproj",
    )(x2, oa, ob, oc, od, w, mod3)


def _ffn_kernel(x_ref, g_ref, sh_ref, sc_ref, gate_ref, w1_ref, w3_ref, w2_ref, o_ref,
                hn_ref, acc_ref):
    f = pl.program_id(1)

    @pl.when(f == 0)
    def _():
        hn_ref[...] = _adaln(x_ref[...], g_ref[...], sh_ref[...], sc_ref[...]).astype(BF16)
        acc_ref[...] = jnp.zeros_like(acc_ref)

    hn = hn_ref[...]
    a = _silu(_dot(hn, w1_ref[...])) * _dot(hn, w3_ref[...])
    acc_ref[...] += _dot(a.astype(BF16), w2_ref[...])

    @pl.when(f == pl.num_programs(1) - 1)
    def _():
        o_ref[...] = x_ref[...] + gate_ref[...] * acc_ref[...]


def _ffn(x2, g, mod3, w1, w3, w2, s_len):
    n, d = x2.shape
    dff = w1.shape[1]
    tm = TM_FFN
    tf = dff // 2
    tpb = s_len // tm
    return pl.pallas_call(
        _ffn_kernel,
        grid=(n // tm, dff // tf),
        in_specs=[
            pl.BlockSpec((tm, d), lambda i, f: (i, 0)),
            pl.BlockSpec((1, d), lambda i, f: (0, 0)),
            pl.BlockSpec((None, 1, d), lambda i, f: (i // tpb, 0, 3)),
            pl.BlockSpec((None, 1, d), lambda i, f: (i // tpb, 0, 4)),
            pl.BlockSpec((None, 1, d), lambda i, f: (i // tpb, 0, 5)),
            pl.BlockSpec((d, tf), lambda i, f: (0, f)),
            pl.BlockSpec((d, tf), lambda i, f: (0, f)),
            pl.BlockSpec((tf, d), lambda i, f: (f, 0)),
        ],
        out_specs=pl.BlockSpec((tm, d), lambda i, f: (i, 0)),
        out_shape=jax.ShapeDtypeStruct((n, d), F32),
        scratch_shapes=[pltpu.VMEM((tm, d), BF16), pltpu.VMEM((tm, d), F32)],
        compiler_params=_cparams(("arbitrary", "arbitrary")),
        name="ffn_dense",
    )(x2, g, mod3, mod3, mod3, w1, w3, w2)


def _router_kernel(x_ref, g_ref, sh_ref, sc_ref, rhi_ref, rlo_ref, tril_ref,
                   hn_ref, comb_ref, pos_ref, post_ref, cum_ref):
    sub = tril_ref.shape[0]
    carry = jnp.zeros((1, LANES), F32)
    cum_ref[...] = jnp.zeros_like(cum_ref)
    for r in range(x_ref.shape[0] // sub):
        rows = slice(r * sub, (r + 1) * sub)
        hn = _adaln(x_ref[rows, :], g_ref[...], sh_ref[...], sc_ref[...])
        hi = hn.astype(BF16)
        hn_ref[rows, :] = hi
        lo = (hn - hi.astype(F32)).astype(BF16)
        logits = _dot(hi, rhi_ref[...]) + (_dot(hi, rlo_ref[...]) + _dot(lo, rhi_ref[...]))
        lane = lax.broadcasted_iota(I32, logits.shape, 1)
        logits = jnp.where(lane < N_EXPERTS, logits, NEG)
        m1 = jnp.max(logits, axis=-1, keepdims=True)
        i1 = jnp.min(jnp.where(logits == m1, lane, LANES), axis=-1, keepdims=True)
        rest = jnp.where(lane == i1, NEG, logits)
        m2 = jnp.max(rest, axis=-1, keepdims=True)
        i2 = jnp.min(jnp.where(rest == m2, lane, LANES), axis=-1, keepdims=True)
        e2 = jnp.exp(m2 - m1)
        g1 = 1.0 / (1.0 + e2)
        g2 = e2 / (1.0 + e2)
        comb_ref[rows, :] = jnp.where(lane == i1, g1, 0.0) + jnp.where(lane == i2, g2, 0.0)
        routed = jnp.where((lane == i1) | (lane == i2), 1.0, 0.0)
        incl = _dot(tril_ref[...], routed.astype(BF16)) + carry
        pos = jnp.where(routed > 0.0, incl - 1.0, -1.0)
        pos_ref[rows, :] = pos
        post_ref[:, rows] = jnp.transpose(pos)[:N_EXPERTS, :]
        carry = carry + jnp.sum(routed, axis=0, keepdims=True)
        cum_ref[r + 1:r + 2, :] = carry


def _router(x2, g, mod3, rhi, rlo, tril, s_len):
    n, d = x2.shape
    tm = min(TM_MOE, s_len)
    tpb = s_len // tm
    ns = n // tm
    return pl.pallas_call(
        _router_kernel,
        grid=(ns,),
        in_specs=[
            pl.BlockSpec((tm, d), lambda i: (i, 0)),
            pl.BlockSpec((1, d), lambda i: (0, 0)),
            pl.BlockSpec((None, 1, d), lambda i: (i // tpb, 0, 3)),
            pl.BlockSpec((None, 1, d), lambda i: (i // tpb, 0, 4)),
            pl.BlockSpec((d, LANES), lambda i: (0, 0)),
            pl.BlockSpec((d, LANES), lambda i: (0, 0)),
            pl.BlockSpec(tril.shape, lambda i: (0, 0)),
        ],
        out_specs=[
            pl.BlockSpec((tm, d), lambda i: (i, 0)),
            pl.BlockSpec((tm, LANES), lambda i: (i, 0)),
            pl.BlockSpec((tm, LANES), lambda i: (i, 0)),
            pl.BlockSpec((None, N_EXPERTS, tm), lambda i: (i, 0, 0)),
            pl.BlockSpec((None, 8, LANES), lambda i: (i, 0, 0)),
        ],
        out_shape=[
            jax.ShapeDtypeStruct((n, d), BF16),
            jax.ShapeDtypeStruct((n, LANES), F32),
            jax.ShapeDtypeStruct((n, LANES), F32),
            jax.ShapeDtypeStruct((ns, N_EXPERTS, tm), F32),
            jax.ShapeDtypeStruct((ns, 8, LANES), F32),
        ],
        compiler_params=_cparams(("arbitrary",)),
        name="router",
    )(x2, g, mod3, mod3, rhi, rlo, tril)


def _moe_kernel(cum_ref, hn_ref, comb_ref, pos_ref, post_ref, w1_ref, w3_ref, w2_ref, o_ref,
                xc_ref, y_ref):
    s, e, f = pl.program_id(0), pl.program_id(1), pl.program_id(2)
    nf = pl.num_programs(2)
    tm = hn_ref.shape[0]
    ch, sub = MOE_CHUNK, MOE_SUB
    nsub = tm // sub
    cum = [cum_ref[(s * (nsub + 1) + r) * N_EXPERTS + e] for r in range(nsub + 1)]
    nch = (cum[nsub] + (ch - 1)) // ch

    @pl.when((e == 0) & (f == 0))
    def _():
        o_ref[...] = jnp.zeros_like(o_ref)

    @pl.when(f == 0)
    def _():
        def clear(c, carry):
            r0 = pl.multiple_of(c * ch, ch)
            xc_ref[pl.ds(r0, ch), :] = jnp.zeros((ch, xc_ref.shape[1]), BF16)
            y_ref[pl.ds(r0, ch), :] = jnp.zeros((ch, y_ref.shape[1]), F32)
            return carry
        lax.fori_loop(0, nch + 1, clear, 0)

        slot = lax.broadcasted_iota(I32, (ch, sub), 0).astype(F32)
        for r in range(nsub):
            prow = post_ref[pl.ds(e, 1), r * sub:(r + 1) * sub]

            def gather(c, carry, r=r, prow=prow):
                r0 = pl.multiple_of(c * ch, ch)
                sel = jnp.where(prow - (c * ch).astype(F32) == slot, 1.0, 0.0).astype(BF16)
                rows = _dot(sel, hn_ref[r * sub:(r + 1) * sub, :])
                xc_ref[pl.ds(r0, ch), :] = (xc_ref[pl.ds(r0, ch), :].astype(F32) + rows).astype(BF16)
                return carry
            lax.fori_loop(cum[r] // ch, (cum[r + 1] + (ch - 1)) // ch, gather, 0)

    def expert(c, carry):
        r0 = pl.multiple_of(c * ch, ch)
        xc = xc_ref[pl.ds(r0, ch), :]
        a = _silu(_dot(xc, w1_ref[...])) * _dot(xc, w3_ref[...])
        y_ref[pl.ds(r0, ch), :] += _dot(a.astype(BF16), w2_ref[...])
        return carry
    lax.fori_loop(0, nch, expert, 0)

    @pl.when(f == nf - 1)
    def _():
        lane = lax.broadcasted_iota(I32, (tm, LANES), 1)
        pcol = jnp.sum(jnp.where(lane == e, pos_ref[...], 0.0), axis=-1, keepdims=True)
        gcol = jnp.sum(jnp.where(lane == e, comb_ref[...], 0.0), axis=-1, keepdims=True)
        slot = lax.broadcasted_iota(I32, (sub, 2 * ch), 1).astype(F32)
        for r in range(nsub):
            rows = slice(r * sub, (r + 1) * sub)

            def scatter(c2, carry, rows=rows):
                r0 = pl.multiple_of(c2 * (2 * ch), 2 * ch)
                z = y_ref[pl.ds(r0, 2 * ch), :].astype(BF16)
                sel = jnp.where(pcol[rows] - (c2 * (2 * ch)).astype(F32) == slot, 1.0, 0.0).astype(BF16)
                o_ref[rows, :] += gcol[rows] * _dot(sel, z)
                return carry
            lax.fori_loop(cum[r] // (2 * ch), (cum[r + 1] + (2 * ch - 1)) // (2 * ch), scatter, 0)


def _moe(hn, comb, pos, post, cum, w1, w3, w2):
    n, d = hn.shape
    ne, _, dff = w1.shape
    ns, _, tm = post.shape
    tf = dff // 4
    cap = tm + 2 * MOE_CHUNK
    grid_spec = pltpu.PrefetchScalarGridSpec(
        num_scalar_prefetch=1,
        grid=(ns, ne, dff // tf),
        in_specs=[
            pl.BlockSpec((tm, d), lambda i, e, f, c: (i, 0)),
            pl.BlockSpec((tm, LANES), lambda i, e, f, c: (i, 0)),
            pl.BlockSpec((tm, LANES), lambda i, e, f, c: (i, 0)),
            pl.BlockSpec((None, N_EXPERTS, tm), lambda i, e, f, c: (i, 0, 0)),
            pl.BlockSpec((None, d, tf), lambda i, e, f, c: (e, 0, f)),
            pl.BlockSpec((None, d, tf), lambda i, e, f, c: (e, 0, f)),
            pl.BlockSpec((None, tf, d), lambda i, e, f, c: (e, f, 0)),
        ],
        out_specs=pl.BlockSpec((tm, d), lambda i, e, f, c: (i, 0)),
        scratch_shapes=[pltpu.VMEM((cap, d), BF16), pltpu.VMEM((cap, d), F32)],
    )
    return pl.pallas_call(
        _moe_kernel,
        grid_spec=grid_spec,
        out_shape=jax.ShapeDtypeStruct((n, d), F32),
        compiler_params=_cparams(("arbitrary", "arbitrary", "arbitrary")),
        name="ffn_experts",
    )(cum, hn, comb, pos, post, w1, w3, w2)


def _residual_kernel(x_ref, f_ref, gate_ref, gn_ref, o_ref, *, final):
    y = x_ref[...] + gate_ref[...] * f_ref[...]
    o_ref[...] = _rms(y, gn_ref[...]) if final else y


def _residual(x2, f2, mod3, gn, s_len, final):
    n, d = x2.shape
    tm = TM_FFN
    tpb = s_len // tm
    row = pl.BlockSpec((tm, d), lambda i: (i, 0))
    return pl.pallas_call(
        functools.partial(_residual_kernel, final=final),
        grid=(n // tm,),
        in_specs=[row, row, pl.BlockSpec((None, 1, d), lambda i: (i // tpb, 0, 5)),
                  pl.BlockSpec((1, d), lambda i: (0, 0))],
        out_specs=row,
        out_shape=jax.ShapeDtypeStruct((n, d), F32),
        compiler_params=_cparams(("arbitrary",)),
        name="residual",
    )(x2, f2, mod3, gn)


def _final_kernel(x_ref, g_ref, o_ref):
    o_ref[...] = _rms(x_ref[...], g_ref[...])


def _final_norm(x2, g):
    n, d = x2.shape
    tm = TM_FFN
    return pl.pallas_call(
        _final_kernel,
        grid=(n // tm,),
        in_specs=[pl.BlockSpec((tm, d), lambda i: (i, 0)), pl.BlockSpec((1, d), lambda i: (0, 0))],
        out_specs=pl.BlockSpec((tm, d), lambda i: (i, 0)),
        out_shape=jax.ShapeDtypeStruct((n, d), F32),
        compiler_params=_cparams(("arbitrary",)),
        name="final_norm",
    )(x2, g)


def _alibi_slopes():
    return 2.0 ** (-8.0 * jnp.arange(1, N_HEADS + 1, dtype=F32) / N_HEADS)


def _block_tables():
    j = jnp.arange(TQ)[:, None]
    i = jnp.arange(TQ)[None, :]
    diagm = jnp.where((j // CHUNK) <= (i // CHUNK), 0.0, NEG).astype(F32)
    corr = -2.0 * _alibi_slopes()[:, None, None] * jnp.maximum(j - i, 0).astype(F32)[None]
    return corr, diagm


def _rope_tables(s_len):
    inv = ROPE_THETA ** (-jnp.arange(0, QK_ROPE, 2, dtype=F32) / QK_ROPE)
    ang = jnp.arange(s_len, dtype=F32)[:, None] * inv[None, :]
    cos2 = jnp.concatenate([jnp.cos(ang)] * 2, axis=-1)
    sin2 = jnp.concatenate([jnp.sin(ang)] * 2, axis=-1)
    z = jnp.zeros((s_len, QK_NOPE), F32)
    z2 = jnp.zeros((s_len, LANES - QK_NOPE - QK_ROPE), F32)
    return (jnp.concatenate([z, cos2, z2], axis=-1), jnp.concatenate([z, sin2, z2], axis=-1))


def _rot_cols(w):
    half = QK_ROPE // 2
    return jnp.concatenate([-w[..., half:], w[..., :half]], axis=-1)


def _slabs(a, nh, extra=None):
    b, s_len, _ = a.shape
    parts = [a.reshape(b, s_len, nh, -1)]
    if extra is not None:
        parts.append(jnp.broadcast_to(extra, (b, s_len, nh, extra.shape[-1])).astype(a.dtype))
    used = sum(p.shape[-1] for p in parts)
    parts.append(jnp.zeros((b, s_len, nh, LANES - used), a.dtype))
    return jnp.concatenate(parts, axis=-1)


def _rows(slab):
    b, s_len = slab.shape[:2]
    return slab.reshape(b, s_len, -1)


def _cols(slab):
    b, s_len = slab.shape[:2]
    return jnp.transpose(slab, (0, 2, 3, 1)).reshape(b, -1, s_len)


def _alibi_extras(s_len, nslab):
    loc = (jnp.arange(s_len) % TQ).astype(F32)[:, None, None]
    sl = jnp.repeat(_alibi_slopes(), nslab // N_HEADS)[None, :, None]
    q_extra = jnp.concatenate([-sl * loc, jnp.broadcast_to(sl, (s_len, nslab, 1))], axis=-1)
    k_extra = jnp.concatenate([jnp.ones((s_len, nslab, 1), F32),
                               jnp.broadcast_to(loc, (s_len, nslab, 1))], axis=-1)
    return q_extra[None], k_extra[None]


def _layer_weights(l, w_in, mla_w_uq, mla_w_ukv):
    w = w_in[l]
    names = ('a_q', 'a_k', 'a_v', 'b_qd', 'b_kvd', 'b_kr', 'c_q', 'c_k', 'c_v', 'c_qi', 'c_ki',
             'c_wi', 'd_q', 'd_k', 'd_v')
    widths = (256, 256, 256, Q_LORA, KV_LORA, QK_ROPE, 256, 256, 256, IDX_HEADS * IDX_DIM, IDX_DIM,
              IDX_HEADS, 256, 256, 256)
    cols, o = {}, 0
    for nme, wd in zip(names, widths):
        cols[nme] = w[:, o:o + wd]
        o += wd
    pad = jnp.zeros((w.shape[0], N_SIDE - (Q_LORA + KV_LORA + 2 * QK_ROPE + IDX_DIM + IDX_HEADS)), F32)
    wcat = jnp.concatenate(
        [cols['a_q'], cols['a_k'], cols['a_v'], cols['c_q'], cols['c_k'], cols['c_v'], cols['c_qi'],
         cols['d_q'], cols['d_k'], cols['d_v'],
         cols['b_qd'], cols['b_kvd'], cols['b_kr'], _rot_cols(cols['b_kr']), cols['c_ki'],
         cols['c_wi'], pad], axis=-1).astype(BF16)
    ones = lambda k: jnp.ones((k,), F32)
    cs = jnp.concatenate([
        ones(256) * DIFF_D ** -0.5, ones(512),
        ones(256) * HEAD_DIM ** -0.5, ones(768),
        ones(256) * HEAD_DIM ** -0.5, ones(512), ones(N_SIDE)])[None, :]

    uq = mla_w_uq[l].reshape(Q_LORA, N_HEADS, QK_NOPE + QK_ROPE)
    zq = jnp.zeros((Q_LORA, N_HEADS, LANES - QK_NOPE - QK_ROPE), F32)
    wqa = jnp.concatenate([uq, zq], axis=-1).reshape(Q_LORA, N_HEADS * LANES).astype(BF16)
    wqb = jnp.concatenate([jnp.zeros((Q_LORA, N_HEADS, QK_NOPE), F32), _rot_cols(uq[..., QK_NOPE:]), zq],
                          axis=-1).reshape(Q_LORA, N_HEADS * LANES).astype(BF16)
    ukv = mla_w_ukv[l].reshape(KV_LORA, N_HEADS, QK_NOPE + HEAD_DIM)
    zk = jnp.zeros((KV_LORA, N_HEADS, LANES - QK_NOPE), F32)
    wk = jnp.concatenate([ukv[..., :QK_NOPE], zk], axis=-1).reshape(KV_LORA, N_HEADS * LANES).astype(BF16)
    wv = jnp.concatenate([ukv[..., QK_NOPE:], zk], axis=-1).reshape(KV_LORA, N_HEADS * LANES).astype(BF16)
    return wcat, cs, wqa, wqb, wk, wv


def _placement():
    r = jnp.arange(LANES)[:, None]
    c = jnp.arange(N_HEADS * LANES)[None, :] % LANES
    e1 = ((c >= QK_NOPE) & (c < QK_NOPE + QK_ROPE) & (r == c - QK_NOPE)).astype(BF16)
    e2 = ((c >= QK_NOPE) & (c < QK_NOPE + QK_ROPE) & (r == c - QK_NOPE + QK_ROPE)).astype(BF16)
    return e1, e2


def kernel(x, c, ada_w, ada_b, mix_norm_g, ffn_norm_g, w_in, w_out, diff_lambda, diff_norm_g, mla_q_norm_g, mla_kv_norm_g, mla_w_uq, mla_w_ukv, band_rel_bias, ffn_w1, ffn_w3, ffn_w2, moe_router, moe_w1, moe_w3, moe_w2, final_norm_g):
    b, s_len, d = x.shape
    depth = ada_w.shape[0]
    n = b * s_len
    n_sel = min(TOPK_MAX, s_len // 4)
    assert d == D_MODEL and s_len % TM_FFN == 0 and s_len % TQ == 0

    mod = _modulation(c, ada_w, ada_b)
    corr, diagm = _block_tables()
    slopes = _alibi_slopes()
    cos_t, sin_t = _rope_tables(s_len)
    e1, e2 = _placement()
    tril = (jnp.arange(TQ)[None, :] <= jnp.arange(TQ)[:, None]).astype(BF16)
    mla_scale = (QK_NOPE + QK_ROPE) ** -0.5
    qx8, kx8 = _alibi_extras(s_len, 2 * N_HEADS)
    qx4, kx4 = _alibi_extras(s_len, N_HEADS)
    one4 = jnp.ones((1, s_len, N_HEADS, 1), F32)
    tr = lambda a: jnp.swapaxes(a, 1, 2)
    front = 2 * TQ

    x2 = x.reshape(n, d)
    for l in range(depth):
        mod3 = mod[l].reshape(b, 1, 6 * d)
        wcat, cs, wqa, wqb, wk, wv = _layer_weights(l, w_in, mla_w_uq, mla_w_ukv)
        main, side = _inproj(x2, mix_norm_g[l][None], mod3, wcat, cs, s_len)
        main3 = main.reshape(b, s_len, N_MAIN)
        side3 = side.reshape(b, s_len, N_SIDE)
        blk = lambda k: main3[:, :, k * 256:(k + 1) * 256]

        lam_init = 0.8 - 0.6 * math.exp(-0.3 * l)
        o_a = _attn_a(_cols(_slabs(blk(0), 2 * N_HEADS, qx8)), _rows(_slabs(blk(1), 2 * N_HEADS, kx8)),
                      _cols(_slabs(blk(2), N_HEADS, one4)), corr + diagm[None], slopes, diff_lambda[l],
                      jnp.broadcast_to(diff_norm_g[l][:, None], (2 * DIFF_D, TQ)), lam_init)

        qb_, kb_, vb_ = _mla_prep(side, mla_q_norm_g[l][None], mla_kv_norm_g[l][None], wqa, wqb, wk, wv,
                                  e1, e2, cos_t, sin_t, s_len, mla_scale)
        o_b = _attn_b(tr(qb_.reshape(b, s_len, -1)), kb_.reshape(b, s_len, -1),
                      tr(vb_.reshape(b, s_len, -1)), diagm)

        o0 = Q_LORA + KV_LORA + 2 * QK_ROPE
        ki = side3[:, :, o0:o0 + IDX_DIM].astype(BF16)
        wt = tr(side3[:, :, o0 + IDX_DIM:o0 + IDX_DIM + IDX_HEADS])
        o_c = _attn_c(_cols(_slabs(blk(3), N_HEADS, qx4)), tr(blk(6)), wt, _rows(_slabs(blk(4), N_HEADS, kx4)),
                      ki, _cols(_slabs(blk(5), N_HEADS, one4)), corr, diagm, slopes, tril, n_sel)

        k_pad = jnp.pad(_rows(_slabs(blk(8), N_HEADS)), ((0, 0), (front, 0), (0, 0)))
        vt_pad = jnp.pad(_cols(_slabs(blk(9), N_HEADS, one4)), ((0, 0), (0, 0), (front, 0)))
        o_d = _attn_d(_cols(_slabs(blk(7), N_HEADS)), k_pad, vt_pad, _band_bias(band_rel_bias[l]))

        flat = lambda a: a.reshape(n, 256)
        x2 = _outproj(x2, flat(o_a), flat(o_b), flat(o_c), flat(o_d), w_out[l].astype(BF16), mod3, s_len)

        gf = ffn_norm_g[l][None]
        if l % 2 == 0:
            i = l // 2
            x2 = _ffn(x2, gf, mod3, ffn_w1[i].astype(BF16), ffn_w3[i].astype(BF16),
                      ffn_w2[i].astype(BF16), s_len)
        else:
            i = l // 2
            r = jnp.pad(moe_router[i], ((0, 0), (0, LANES - N_EXPERTS)))
            rhi = r.astype(BF16)
            rlo = (r - rhi.astype(F32)).astype(BF16)
            tril_sub = (jnp.arange(MOE_SUB)[None, :] <= jnp.arange(MOE_SUB)[:, None]).astype(BF16)
            hn, comb, pos, post, cum = _router(x2, gf, mod3, rhi, rlo, tril_sub, s_len)
            nsub = post.shape[-1] // MOE_SUB
            cum_i = cum[:, :nsub + 1, :N_EXPERTS].astype(I32).reshape(-1)
            f2 = _moe(hn, comb, pos, post, cum_i, moe_w1[i].astype(BF16), moe_w3[i].astype(BF16),
                      moe_w2[i].astype(BF16))
            last = l == depth - 1
            x2 = _residual(x2, f2, mod3, final_norm_g[None], s_len, last)
            if last:
                return x2.reshape(b, s_len, d)
    return _final_norm(x2, final_norm_g[None]).reshape(b, s_len, d)
```

```python
import functools
import math

import numpy as np
import jax
import jax.numpy as jnp
from jax import lax
from jax.experimental import pallas as pl
from jax.experimental.pallas import tpu as pltpu

F32 = jnp.float32
BF16 = jnp.bfloat16
I32 = jnp.int32
I16 = jnp.int16

D_MODEL = 1024
CHUNK = 64
N_HEADS = 4
HEAD_DIM = 64
DIFF_D = 32
Q_LORA = 256
KV_LORA = 128
QK_NOPE = 64
QK_ROPE = 32
ROPE_THETA = 10000.0
IDX_HEADS = 8
IDX_DIM = 32
TOPK_MAX = 256
BAND_CHUNKS = 8
REL_CLIP = 128
N_EXPERTS = 8
EPS = 1e-6

NEG = -1e30
INT_MIN = -(2 ** 31)
I16_MIN = -(2 ** 15)

LANES = 128
PACK16 = 16
VMEM_LIMIT = 56 * 1024 * 1024
TQ = 256
SWEEP_UNITS = 16
TM_PROJ = 512
TM_FFN = 512
TM_MOE = 2048
MOE_CHUNK = 128
MOE_SUB = 512
N_MAIN = 2560
N_SIDE = 512


def _cparams(sem):
    return pltpu.CompilerParams(dimension_semantics=sem, vmem_limit_bytes=VMEM_LIMIT)


def _dot(a, b):
    return jnp.dot(a, b, preferred_element_type=F32)


def _silu(x):
    return x / (1.0 + jnp.exp(-x))


def _rms(x, g):
    return x * lax.rsqrt(jnp.mean(x * x, axis=-1, keepdims=True) + EPS) * g


def _adaln(x, g, shift, scale):
    return _rms(x, g) * (1.0 + scale) + shift


def _mod_kernel(c_ref, w_ref, b_ref, o_ref):
    sc = _silu(c_ref[...]).astype(BF16)
    o_ref[...] = _dot(sc, w_ref[...].astype(BF16)) + b_ref[...]


def _modulation(c, ada_w, ada_b):
    depth, d, n6 = ada_w.shape
    b = c.shape[0]
    tn = 1536
    return pl.pallas_call(
        _mod_kernel,
        grid=(depth, n6 // tn),
        in_specs=[
            pl.BlockSpec((b, d), lambda l, j: (0, 0)),
            pl.BlockSpec((None, d, tn), lambda l, j: (l, 0, j)),
            pl.BlockSpec((None, 1, tn), lambda l, j: (l, 0, j)),
        ],
        out_specs=pl.BlockSpec((None, b, tn), lambda l, j: (l, 0, j)),
        out_shape=jax.ShapeDtypeStruct((depth, b, n6), F32),
        compiler_params=_cparams(("arbitrary", "arbitrary")),
        name="modulation",
    )(c, ada_w, ada_b.reshape(depth, 1, n6))


def _inproj_kernel(x_ref, g_ref, sh_ref, sc_ref, w_ref, cs_ref, main_ref, side_ref):
    hn = _adaln(x_ref[...], g_ref[...], sh_ref[...], sc_ref[...]).astype(BF16)
    step = 512
    for c0 in range(0, N_MAIN + N_SIDE, step):
        acc = _dot(hn, w_ref[:, c0:c0 + step]) * cs_ref[:, c0:c0 + step]
        if c0 < N_MAIN:
            main_ref[:, c0:c0 + step] = acc.astype(BF16)
        else:
            side_ref[:, c0 - N_MAIN:c0 - N_MAIN + step] = acc


def _inproj(x2, g, mod3, w, cs, s_len):
    n, d = x2.shape
    tm = TM_PROJ
    tpb = s_len // tm
    nc = N_MAIN + N_SIDE
    return pl.pallas_call(
        _inproj_kernel,
        grid=(n // tm,),
        in_specs=[
            pl.BlockSpec((tm, d), lambda i: (i, 0)),
            pl.BlockSpec((1, d), lambda i: (0, 0)),
            pl.BlockSpec((None, 1, d), lambda i: (i // tpb, 0, 0)),
            pl.BlockSpec((None, 1, d), lambda i: (i // tpb, 0, 1)),
            pl.BlockSpec((d, nc), lambda i: (0, 0)),
            pl.BlockSpec((1, nc), lambda i: (0, 0)),
        ],
        out_specs=[
            pl.BlockSpec((tm, N_MAIN), lambda i: (i, 0)),
            pl.BlockSpec((tm, N_SIDE), lambda i: (i, 0)),
        ],
        out_shape=[
            jax.ShapeDtypeStruct((n, N_MAIN), BF16),
            jax.ShapeDtypeStruct((n, N_SIDE), F32),
        ],
        compiler_params=_cparams(("arbitrary",)),
        name="inproj",
    )(x2, g, mod3, mod3, w, cs)


def _softmax_update(s, c, m, acc_ref, slot, vt):
    m_new = jnp.maximum(m, jnp.max(s, axis=0, keepdims=True) + c)
    alpha = jnp.exp(m - m_new)
    p = jnp.exp(s - (m_new - c))
    acc_ref[slot] = acc_ref[slot] * alpha + _dot(vt, p.astype(BF16))
    return m_new


def _normalized(acc_ref, slot):
    a = acc_ref[slot]
    return a[:HEAD_DIM] / a[HEAD_DIM:HEAD_DIM + 1]


def _pipelined(n, produce, consume, lookahead=3):
    vals = [produce(i) for i in range(min(lookahead, n))]
    for i in range(n):
        if i + lookahead < n:
            vals.append(produce(i + lookahead))
        consume(i, vals[i])
        vals[i] = None


def _sweep(qb, body, carry, group):
    ng = qb // group
    carry = lax.fori_loop(0, ng, lambda g, c: body(g * group, c, group), carry)
    return lax.fori_loop(ng * group, qb, lambda kb, c: body(kb, c, 1), carry)


def _block_const(slope_ref, h, qb, kb, tq):
    return -slope_ref[h] * ((qb - kb) * tq).astype(F32)


def _attn_a_kernel(qt_ref, k_ref, vt_ref, corr_ref, slope_ref, lamv_ref, g_ref, o_ref, acc_ref,
                   *, lam_init):
    qb = pl.program_id(1)
    tq = o_ref.shape[0]
    nhm = 2 * N_HEADS
    acc_ref[...] = jnp.zeros_like(acc_ref)

    def block(kb, ms, nb, diag=False):
        out = list(ms)
        start = lambda u: pl.multiple_of((kb + u // nhm) * tq, tq)

        def logits(u):
            i = u % nhm
            return _dot(k_ref[pl.ds(start(u), tq), i * LANES:(i + 1) * LANES], qt_ref[i * LANES:(i + 1) * LANES, :])

        def update(u, s):
            i = u % nhm
            h = i // 2
            vt = vt_ref[h * LANES:(h + 1) * LANES, pl.ds(start(u), tq)]
            if diag:
                s, c = corr_ref[h] + s, 0.0
            else:
                c = _block_const(slope_ref, h, qb, kb + u // nhm, tq)
            out[i] = _softmax_update(s, c, out[i], acc_ref, i, vt)

        _pipelined(nb * nhm, logits, update)
        return tuple(out)

    ms = tuple(jnp.full((1, tq), NEG, F32) for _ in range(nhm))
    ms = block(qb, ms, 1, diag=True)
    _sweep(qb, block, ms, SWEEP_UNITS // nhm)

    lv = lamv_ref[...]
    lam = (jnp.exp(jnp.sum(lv[0:1] * lv[1:2], axis=-1, keepdims=True))
           - jnp.exp(jnp.sum(lv[2:3] * lv[3:4], axis=-1, keepdims=True)) + lam_init)
    outs = []
    for h in range(N_HEADS):
        o = _normalized(acc_ref, 2 * h) - lam * _normalized(acc_ref, 2 * h + 1)
        o = o * lax.rsqrt(jnp.mean(o * o, axis=0, keepdims=True) + EPS) * g_ref[...]
        outs.append(o * (1.0 - lam_init))
    o_ref[...] = jnp.transpose(jnp.concatenate(outs, axis=0)).astype(o_ref.dtype)


def _attn_a(qt, k, vt, corr, slopes, lamv, gcol, lam_init):
    b, s_len, _ = k.shape
    nq = s_len // TQ
    return pl.pallas_call(
        functools.partial(_attn_a_kernel, lam_init=lam_init),
        grid=(b, nq),
        in_specs=[
            pl.BlockSpec((None, 2 * N_HEADS * LANES, TQ), lambda i, j: (i, 0, j)),
            pl.BlockSpec((None, s_len, 2 * N_HEADS * LANES), lambda i, j: (i, 0, 0)),
            pl.BlockSpec((None, N_HEADS * LANES, s_len), lambda i, j: (i, 0, 0)),
            pl.BlockSpec((N_HEADS, TQ, TQ), lambda i, j: (0, 0, 0)),
            pl.BlockSpec(memory_space=pltpu.SMEM),
            pl.BlockSpec((4, DIFF_D), lambda i, j: (0, 0)),
            pl.BlockSpec((2 * DIFF_D, TQ), lambda i, j: (0, 0)),
        ],
        out_specs=pl.BlockSpec((None, TQ, 256), lambda i, j: (i, j, 0)),
        out_shape=jax.ShapeDtypeStruct((b, s_len, 256), BF16),
        scratch_shapes=[pltpu.VMEM((2 * N_HEADS, LANES, TQ), F32)],
        compiler_params=_cparams(("arbitrary", "arbitrary")),
        name="attn_diff",
    )(qt, k, vt, corr, slopes, lamv, gcol)


def _mla_prep_kernel(side_ref, gq_ref, gkv_ref, wqa_ref, wqb_ref, wk_ref, wv_ref, e1_ref, e2_ref,
                     cos_ref, sin_ref, q_ref, k_ref, v_ref, *, qscale):
    side = side_ref[...]
    hq = _rms(side[:, :Q_LORA], gq_ref[...]).astype(BF16)
    hkv = _rms(side[:, Q_LORA:Q_LORA + KV_LORA], gkv_ref[...]).astype(BF16)
    misc = side[:, Q_LORA + KV_LORA:]
    cos4 = jnp.concatenate([cos_ref[...]] * N_HEADS, axis=-1)
    sin4 = jnp.concatenate([sin_ref[...]] * N_HEADS, axis=-1)
    lane = lax.broadcasted_iota(I32, (1, N_HEADS * LANES), 1) % LANES
    nope = (lane < QK_NOPE).astype(F32)
    ones_col = (lane == HEAD_DIM).astype(F32)
    q = (_dot(hq, wqa_ref[...]) * (nope + cos4) + _dot(hq, wqb_ref[...]) * sin4) * qscale
    q_ref[...] = q.astype(BF16)
    hi = misc.astype(BF16)
    lo = (misc - hi.astype(F32)).astype(BF16)
    kr = _dot(hi, e1_ref[...]) + _dot(lo, e1_ref[...])
    krp = _dot(hi, e2_ref[...]) + _dot(lo, e2_ref[...])
    k = _dot(hkv, wk_ref[...]) + kr * cos4 + krp * sin4
    k_ref[...] = k.astype(BF16)
    v_ref[...] = (_dot(hkv, wv_ref[...]) + ones_col).astype(BF16)


def _mla_prep(side, gq, gkv, wqa, wqb, wk, wv, e1, e2, cos_t, sin_t, s_len, qscale):
    n = side.shape[0]
    tm = TM_PROJ
    tpb = s_len // tm
    full = lambda a: pl.BlockSpec(a.shape, lambda i: (0,) * a.ndim)
    out = jax.ShapeDtypeStruct((n, N_HEADS * LANES), BF16)
    return pl.pallas_call(
        functools.partial(_mla_prep_kernel, qscale=qscale),
        grid=(n // tm,),
        in_specs=[pl.BlockSpec((tm, N_SIDE), lambda i: (i, 0)),
                  full(gq), full(gkv), full(wqa), full(wqb), full(wk), full(wv), full(e1), full(e2),
                  pl.BlockSpec((tm, LANES), lambda i: (i % tpb, 0)),
                  pl.BlockSpec((tm, LANES), lambda i: (i % tpb, 0))],
        out_specs=[pl.BlockSpec((tm, N_HEADS * LANES), lambda i: (i, 0))] * 3,
        out_shape=[out, out, out],
        compiler_params=_cparams(("arbitrary",)),
        name="mla_prep",
    )(side, gq, gkv, wqa, wqb, wk, wv, e1, e2, cos_t, sin_t)


def _attn_b_kernel(qt_ref, k_ref, vt_ref, diagm_ref, o_ref, acc_ref):
    qb = pl.program_id(1)
    tq = o_ref.shape[0]
    acc_ref[...] = jnp.zeros_like(acc_ref)

    def block(kb, ms, nb, diag=False):
        out = list(ms)
        start = lambda u: pl.multiple_of((kb + u // N_HEADS) * tq, tq)

        def logits(u):
            h = u % N_HEADS
            return _dot(k_ref[pl.ds(start(u), tq), h * LANES:(h + 1) * LANES], qt_ref[h * LANES:(h + 1) * LANES, :])

        def update(u, s):
            h = u % N_HEADS
            vt = vt_ref[h * LANES:(h + 1) * LANES, pl.ds(start(u), tq)]
            if diag:
                s = diagm_ref[...] + s
            out[h] = _softmax_update(s, 0.0, out[h], acc_ref, h, vt)

        _pipelined(nb * N_HEADS, logits, update)
        return tuple(out)

    ms = tuple(jnp.full((1, tq), NEG, F32) for _ in range(N_HEADS))
    ms = block(qb, ms, 1, diag=True)
    _sweep(qb, block, ms, SWEEP_UNITS // N_HEADS)
    o_ref[...] = jnp.transpose(jnp.concatenate([_normalized(acc_ref, h) for h in range(N_HEADS)],
                                               axis=0)).astype(o_ref.dtype)


def _attn_b(qt, k, vt, diagm):
    b, s_len, _ = k.shape
    nq = s_len // TQ
    return pl.pallas_call(
        _attn_b_kernel,
        grid=(b, nq),
        in_specs=[
            pl.BlockSpec((None, N_HEADS * LANES, TQ), lambda i, j: (i, 0, j)),
            pl.BlockSpec((None, s_len, N_HEADS * LANES), lambda i, j: (i, 0, 0)),
            pl.BlockSpec((None, N_HEADS * LANES, s_len), lambda i, j: (i, 0, 0)),
            pl.BlockSpec((TQ, TQ), lambda i, j: (0, 0)),
        ],
        out_specs=pl.BlockSpec((None, TQ, 256), lambda i, j: (i, j, 0)),
        out_shape=jax.ShapeDtypeStruct((b, s_len, 256), BF16),
        scratch_shapes=[pltpu.VMEM((N_HEADS, LANES, TQ), F32)],
        compiler_params=_cparams(("arbitrary", "arbitrary")),
        name="attn_latent",
    )(qt, k, vt, diagm)


def _attn_c_kernel(qt_ref, qit_ref, wt_ref, k_ref, ki_ref, vt_ref, corr_ref, diagm_ref, slope_ref,
                   tril_ref, o_ref, keys_ref, khi_ref, klo_ref, acc_ref, *, n_sel):
    qb = pl.program_id(1)
    tq = o_ref.shape[0]
    nkb = qb + 1

    wt = wt_ref[...] * (IDX_HEADS ** -0.5 * IDX_DIM ** -0.5)

    def score_block(kb, diag):
        k0 = pl.multiple_of(kb * tq, tq)
        ki = ki_ref[pl.ds(k0, tq), :]
        total = [jnp.zeros((tq, tq), F32)]

        def weighted(h, d):
            total[0] = total[0] + jnp.maximum(d, 0.0) * wt[h:h + 1]

        _pipelined(IDX_HEADS, lambda h: _dot(ki, qit_ref[h * IDX_DIM:(h + 1) * IDX_DIM, :]), weighted)
        sc = jnp.where(total[0] == 0.0, 0.0, total[0])
        bits = lax.bitcast_convert_type(sc, I32)
        key = bits ^ ((bits >> 31) & 0x7FFFFFFF)
        if diag:
            key = jnp.where(diagm_ref[...] < 0.0, INT_MIN, key)
        keys_ref[pl.ds(k0, tq), :] = key
        khi_ref[pl.ds(k0, tq), :] = (key >> 16).astype(I16)
        klo_ref[pl.ds(k0, tq), :] = ((key & 0xFFFF) - 32768).astype(I16)

    score_block(qb, True)

    def _score_loop(kb, c):
        score_block(kb, False)
        return c
    lax.fori_loop(0, qb, _score_loop, 0)

    def count_ge(cand):
        def body(kb, cnt8):
            k0 = pl.multiple_of(kb * tq, tq)
            hit = jnp.where(keys_ref[pl.ds(k0, tq), :] >= cand, 1, 0)
            return cnt8 + jnp.sum(hit.reshape(tq // 8, 8, tq), axis=0)
        cnt8 = lax.fori_loop(0, nkb, body, jnp.zeros((8, tq), I32))
        return jnp.sum(cnt8, axis=0, keepdims=True)

    def count16(ref, cand):
        cand16 = cand.astype(I16)

        def body(kb, cnt):
            k0 = pl.multiple_of(kb * tq, tq)
            hit = jnp.where(ref[pl.ds(k0, tq), :] >= cand16, jnp.int16(1), jnp.int16(0))
            for j in range(tq // PACK16):
                cnt = cnt + hit[j * PACK16:(j + 1) * PACK16]
            return cnt
        cnt = lax.fori_loop(0, nkb, body, jnp.zeros((PACK16, tq), I16))
        return jnp.sum(cnt.astype(I32), axis=0, keepdims=True)

    def select16(ref, k):
        zero = jnp.zeros((1, tq), I32)
        t = jnp.where(count16(ref, zero) >= k, zero, I16_MIN)

        def bit_body(i, t):
            cand = t + (jnp.int32(1) << (14 - i))
            return jnp.where(count16(ref, cand) >= k, cand, t)
        return lax.fori_loop(0, 15, bit_body, t)

    thr_hi = select16(khi_ref, n_sel)
    above = jnp.where(thr_hi >= 32767, 0, count16(khi_ref, thr_hi + 1))
    thr_hi16 = thr_hi.astype(I16)

    def _bucket(kb, c):
        k0 = pl.multiple_of(kb * tq, tq)
        klo_ref[pl.ds(k0, tq), :] = jnp.where(khi_ref[pl.ds(k0, tq), :] == thr_hi16,
                                              klo_ref[pl.ds(k0, tq), :], jnp.int16(I16_MIN))
        return c
    lax.fori_loop(0, nkb, _bucket, 0)
    thr_lo = select16(klo_ref, n_sel - above)
    thr = thr_hi * 65536 + (thr_lo + 32768)
    thr = jnp.maximum(thr, INT_MIN + 1)
    c_gt = count_ge(thr + 1)
    c_ge = count_ge(thr)
    need = (n_sel - c_gt).astype(F32)
    has_ties = jnp.max(c_ge) > n_sel

    def attend(kb, carry, nb, diag=False, ties=False):
        ms, eq_before = carry
        out = list(ms)
        start = lambda j: pl.multiple_of((kb + j) * tq, tq)
        sels = []
        for j in range(nb):
            key = keys_ref[pl.ds(start(j), tq), :]
            if ties:
                eq = jnp.where(key == thr, 1.0, 0.0)
                rank = _dot(tril_ref[...], eq.astype(BF16)) + eq_before
                sels.append((key > thr) | ((key == thr) & (rank <= need)))
                eq_before = eq_before + jnp.sum(eq, axis=0, keepdims=True)
            else:
                sels.append(key >= thr)

        def logits(u):
            h, j = u % N_HEADS, u // N_HEADS
            return _dot(k_ref[pl.ds(start(j), tq), h * LANES:(h + 1) * LANES], qt_ref[h * LANES:(h + 1) * LANES, :])

        def update(u, s):
            h, j = u % N_HEADS, u // N_HEADS
            vt = vt_ref[h * LANES:(h + 1) * LANES, pl.ds(start(j), tq)]
            if diag:
                s, c = corr_ref[h] + s, 0.0
            else:
                c = _block_const(slope_ref, h, qb, kb + j, tq)
            s = jnp.where(sels[j], s, NEG)
            out[h] = _softmax_update(s, c, out[h], acc_ref, h, vt)

        _pipelined(nb * N_HEADS, logits, update)
        return tuple(out), eq_before

    def run(ties):
        acc_ref[...] = jnp.zeros_like(acc_ref)
        carry = (tuple(jnp.full((1, tq), NEG, F32) for _ in range(N_HEADS)), jnp.zeros((1, tq), F32))
        carry = _sweep(qb, lambda kb, c, nb: attend(kb, c, nb, ties=ties), carry, SWEEP_UNITS // N_HEADS)
        attend(qb, carry, 1, diag=True, ties=ties)

    @pl.when(has_ties)
    def _():
        run(True)

    @pl.when(jnp.logical_not(has_ties))
    def _():
        run(False)

    o_ref[...] = jnp.transpose(jnp.concatenate([_normalized(acc_ref, h) for h in range(N_HEADS)],
                                               axis=0)).astype(o_ref.dtype)


def _attn_c(qt, qit, wt, k, ki, vt, corr, diagm, slopes, tril, n_sel):
    b, s_len, _ = k.shape
    nq = s_len // TQ
    return pl.pallas_call(
        functools.partial(_attn_c_kernel, n_sel=n_sel),
        grid=(b, nq),
        in_specs=[
            pl.BlockSpec((None, N_HEADS * LANES, TQ), lambda i, j: (i, 0, j)),
            pl.BlockSpec((None, IDX_HEADS * IDX_DIM, TQ), lambda i, j: (i, 0, j)),
            pl.BlockSpec((None, IDX_HEADS, TQ), lambda i, j: (i, 0, j)),
            pl.BlockSpec((None, s_len, N_HEADS * LANES), lambda i, j: (i, 0, 0)),
            pl.BlockSpec((None, s_len, IDX_DIM), lambda i, j: (i, 0, 0)),
            pl.BlockSpec((None, N_HEADS * LANES, s_len), lambda i, j: (i, 0, 0)),
            pl.BlockSpec((N_HEADS, TQ, TQ), lambda i, j: (0, 0, 0)),
            pl.BlockSpec((TQ, TQ), lambda i, j: (0, 0)),
            pl.BlockSpec(memory_space=pltpu.SMEM),
            pl.BlockSpec((TQ, TQ), lambda i, j: (0, 0)),
        ],
        out_specs=pl.BlockSpec((None, TQ, 256), lambda i, j: (i, j, 0)),
        out_shape=jax.ShapeDtypeStruct((b, s_len, 256), BF16),
        scratch_shapes=[pltpu.VMEM((s_len, TQ), I32),
                        pltpu.VMEM((s_len, TQ), I16),
                        pltpu.VMEM((s_len, TQ), I16),
                        pltpu.VMEM((N_HEADS, LANES, TQ), F32)],
        compiler_params=_cparams(("arbitrary", "arbitrary")),
        name="attn_sparse",
    )(qt, qit, wt, k, ki, vt, corr, diagm, slopes, tril)


def _band_bias_kernel(rb_ref, o_ref):
    h = pl.program_id(0)
    nk, tq = o_ref.shape
    j = lax.broadcasted_iota(I32, (nk, tq), 0)
    t = lax.broadcasted_iota(I32, (nk, tq), 1) + (nk - tq)
    idx = jnp.clip(t - j, -REL_CLIP, REL_CLIP) + REL_CLIP
    cq, ck = t // CHUNK, j // CHUNK
    valid = (ck <= cq) & (ck >= cq - BAND_CHUNKS)
    tbl = lax.fori_loop(0, 2 * REL_CLIP + 1,
                        lambda r, tb: jnp.where(idx == r, rb_ref[h, r], tb), jnp.zeros((nk, tq), F32))
    o_ref[...] = jnp.where(valid, tbl, NEG)


def _band_bias(rel_bias):
    nh = rel_bias.shape[0]
    return pl.pallas_call(
        _band_bias_kernel,
        grid=(nh,),
        in_specs=[pl.BlockSpec(memory_space=pltpu.SMEM)],
        out_specs=pl.BlockSpec((None, 3 * TQ, TQ), lambda h: (h, 0, 0)),
        out_shape=jax.ShapeDtypeStruct((nh, 3 * TQ, TQ), F32),
        compiler_params=_cparams(("arbitrary",)),
        name="band_bias",
    )(rel_bias)


def _attn_d_kernel(qt_ref, k_ref, vt_ref, bias_ref, o_ref):
    qb = pl.program_id(1)
    tq = o_ref.shape[0]
    outs = [None] * N_HEADS

    def logits(h):
        ds = []
        for j in range(3):
            k0 = pl.multiple_of((qb + j) * tq, tq)
            ds.append(_dot(k_ref[pl.ds(k0, tq), h * LANES:(h + 1) * LANES], qt_ref[h * LANES:(h + 1) * LANES, :]))
        return ds

    def attend(h, ds):
        ss = []
        for j in range(3):
            s = bias_ref[h, j * tq:(j + 1) * tq, :] + ds[j]
            ss.append(jnp.where(qb + j >= 2, s, NEG))
        m = jnp.maximum(jnp.maximum(jnp.max(ss[0], axis=0, keepdims=True),
                                    jnp.max(ss[1], axis=0, keepdims=True)),
                        jnp.max(ss[2], axis=0, keepdims=True))
        acc = jnp.zeros((LANES, tq), F32)
        for j in range(3):
            k0 = pl.multiple_of((qb + j) * tq, tq)
            vt = vt_ref[h * LANES:(h + 1) * LANES, pl.ds(k0, tq)]
            acc = acc + _dot(vt, jnp.exp(ss[j] - m).astype(BF16))
        outs[h] = acc[:HEAD_DIM] / acc[HEAD_DIM:HEAD_DIM + 1]

    _pipelined(N_HEADS, logits, attend, lookahead=1)
    o_ref[...] = jnp.transpose(jnp.concatenate(outs, axis=0)).astype(o_ref.dtype)


def _attn_d(qt, k_pad, vt_pad, bias):
    b, sp, _ = k_pad.shape
    s_len = sp - 2 * TQ
    nq = s_len // TQ
    return pl.pallas_call(
        _attn_d_kernel,
        grid=(b, nq),
        in_specs=[
            pl.BlockSpec((None, N_HEADS * LANES, TQ), lambda i, j: (i, 0, j)),
            pl.BlockSpec((None, sp, N_HEADS * LANES), lambda i, j: (i, 0, 0)),
            pl.BlockSpec((None, N_HEADS * LANES, sp), lambda i, j: (i, 0, 0)),
            pl.BlockSpec((N_HEADS, 3 * TQ, TQ), lambda i, j: (0, 0, 0)),
        ],
        out_specs=pl.BlockSpec((None, TQ, 256), lambda i, j: (i, j, 0)),
        out_shape=jax.ShapeDtypeStruct((b, s_len, 256), BF16),
        compiler_params=_cparams(("arbitrary", "arbitrary")),
        name="attn_band",
    )(qt, k_pad, vt_pad, bias)


def _outproj_kernel(x_ref, oa_ref, ob_ref, oc_ref, od_ref, w_ref, gate_ref, o_ref):
    y = _dot(oa_ref[...], w_ref[0:256, :])
    y = y + _dot(ob_ref[...], w_ref[256:512, :])
    y = y + _dot(oc_ref[...], w_ref[512:768, :])
    y = y + _dot(od_ref[...], w_ref[768:1024, :])
    o_ref[...] = x_ref[...] + gate_ref[...] * y


def _outproj(x2, oa, ob, oc, od, w, mod3, s_len):
    n, d = x2.shape
    tm = TM_PROJ
    tpb = s_len // tm
    ospec = pl.BlockSpec((tm, 256), lambda i: (i, 0))
    return pl.pallas_call(
        _outproj_kernel,
        grid=(n // tm,),
        in_specs=[pl.BlockSpec((tm, d), lambda i: (i, 0)), ospec, ospec, ospec, ospec,
                  pl.BlockSpec((d, d), lambda i: (0, 0)),
                  pl.BlockSpec((None, 1, d), lambda i: (i // tpb, 0, 2))],
        out_specs=pl.BlockSpec((tm, d), lambda i: (i, 0)),
        out_shape=jax.ShapeDtypeStruct((n, d), F32),
        compiler_params=_cparams(("arbitrary",)),
        name="outproj",
    )(x2, oa, ob, oc, od, w, mod3)


def _ffn_kernel(x_ref, g_ref, sh_ref, sc_ref, gate_ref, w1_ref, w3_ref, w2_ref, o_ref,
                hn_ref, acc_ref):
    f = pl.program_id(1)

    @pl.when(f == 0)
    def _():
        hn_ref[...] = _adaln(x_ref[...], g_ref[...], sh_ref[...], sc_ref[...]).astype(BF16)
        acc_ref[...] = jnp.zeros_like(acc_ref)

    hn = hn_ref[...]
    a = _silu(_dot(hn, w1_ref[...])) * _dot(hn, w3_ref[...])
    acc_ref[...] += _dot(a.astype(BF16), w2_ref[...])

    @pl.when(f == pl.num_programs(1) - 1)
    def _():
        o_ref[...] = x_ref[...] + gate_ref[...] * acc_ref[...]


def _ffn(x2, g, mod3, w1, w3, w2, s_len):
    n, d = x2.shape
    dff = w1.shape[1]
    tm = TM_FFN
    tf = dff // 2
    tpb = s_len // tm
    return pl.pallas_call(
        _ffn_kernel,
        grid=(n // tm, dff // tf),
        in_specs=[
            pl.BlockSpec((tm, d), lambda i, f: (i, 0)),
            pl.BlockSpec((1, d), lambda i, f: (0, 0)),
            pl.BlockSpec((None, 1, d), lambda i, f: (i // tpb, 0, 3)),
            pl.BlockSpec((None, 1, d), lambda i, f: (i // tpb, 0, 4)),
            pl.BlockSpec((None, 1, d), lambda i, f: (i // tpb, 0, 5)),
            pl.BlockSpec((d, tf), lambda i, f: (0, f)),
            pl.BlockSpec((d, tf), lambda i, f: (0, f)),
            pl.BlockSpec((tf, d), lambda i, f: (f, 0)),
        ],
        out_specs=pl.BlockSpec((tm, d), lambda i, f: (i, 0)),
        out_shape=jax.ShapeDtypeStruct((n, d), F32),
        scratch_shapes=[pltpu.VMEM((tm, d), BF16), pltpu.VMEM((tm, d), F32)],
        compiler_params=_cparams(("arbitrary", "arbitrary")),
        name="ffn_dense",
    )(x2, g, mod3, mod3, mod3, w1, w3, w2)


def _router_kernel(x_ref, g_ref, sh_ref, sc_ref, rhi_ref, rlo_ref, tril_ref,
                   hn_ref, comb_ref, pos_ref, post_ref, cum_ref):
    sub = tril_ref.shape[0]
    carry = jnp.zeros((1, LANES), F32)
    cum_ref[...] = jnp.zeros_like(cum_ref)
    for r in range(x_ref.shape[0] // sub):
        rows = slice(r * sub, (r + 1) * sub)
        hn = _adaln(x_ref[rows, :], g_ref[...], sh_ref[...], sc_ref[...])
        hi = hn.astype(BF16)
        hn_ref[rows, :] = hi
        lo = (hn - hi.astype(F32)).astype(BF16)
        logits = _dot(hi, rhi_ref[...]) + (_dot(hi, rlo_ref[...]) + _dot(lo, rhi_ref[...]))
        lane = lax.broadcasted_iota(I32, logits.shape, 1)
        logits = jnp.where(lane < N_EXPERTS, logits, NEG)
        m1 = jnp.max(logits, axis=-1, keepdims=True)
        i1 = jnp.min(jnp.where(logits == m1, lane, LANES), axis=-1, keepdims=True)
        rest = jnp.where(lane == i1, NEG, logits)
        m2 = jnp.max(rest, axis=-1, keepdims=True)
        i2 = jnp.min(jnp.where(rest == m2, lane, LANES), axis=-1, keepdims=True)
        e2 = jnp.exp(m2 - m1)
        g1 = 1.0 / (1.0 + e2)
        g2 = e2 / (1.0 + e2)
        comb_ref[rows, :] = jnp.where(lane == i1, g1, 0.0) + jnp.where(lane == i2, g2, 0.0)
        routed = jnp.where((lane == i1) | (lane == i2), 1.0, 0.0)
        incl = _dot(tril_ref[...], routed.astype(BF16)) + carry
        pos = jnp.where(routed > 0.0, incl - 1.0, -1.0)
        pos_ref[rows, :] = pos
        post_ref[:, rows] = jnp.transpose(pos)[:N_EXPERTS, :]
        carry = carry + jnp.sum(routed, axis=0, keepdims=True)
        cum_ref[r + 1:r + 2, :] = carry


def _router(x2, g, mod3, rhi, rlo, tril, s_len):
    n, d = x2.shape
    tm = min(TM_MOE, s_len)
    tpb = s_len // tm
    ns = n // tm
    return pl.pallas_call(
        _router_kernel,
        grid=(ns,),
        in_specs=[
            pl.BlockSpec((tm, d), lambda i: (i, 0)),
            pl.BlockSpec((1, d), lambda i: (0, 0)),
            pl.BlockSpec((None, 1, d), lambda i: (i // tpb, 0, 3)),
            pl.BlockSpec((None, 1, d), lambda i: (i // tpb, 0, 4)),
            pl.BlockSpec((d, LANES), lambda i: (0, 0)),
            pl.BlockSpec((d, LANES), lambda i: (0, 0)),
            pl.BlockSpec(tril.shape, lambda i: (0, 0)),
        ],
        out_specs=[
            pl.BlockSpec((tm, d), lambda i: (i, 0)),
            pl.BlockSpec((tm, LANES), lambda i: (i, 0)),
            pl.BlockSpec((tm, LANES), lambda i: (i, 0)),
            pl.BlockSpec((None, N_EXPERTS, tm), lambda i: (i, 0, 0)),
            pl.BlockSpec((None, 8, LANES), lambda i: (i, 0, 0)),
        ],
        out_shape=[
            jax.ShapeDtypeStruct((n, d), BF16),
            jax.ShapeDtypeStruct((n, LANES), F32),
            jax.ShapeDtypeStruct((n, LANES), F32),
            jax.ShapeDtypeStruct((ns, N_EXPERTS, tm), F32),
            jax.ShapeDtypeStruct((ns, 8, LANES), F32),
        ],
        compiler_params=_cparams(("arbitrary",)),
        name="router",
    )(x2, g, mod3, mod3, rhi, rlo, tril)


def _moe_kernel(cum_ref, hn_ref, comb_ref, pos_ref, post_ref, w1_ref, w3_ref, w2_ref, o_ref,
                xc_ref, y_ref):
    s, e, f = pl.program_id(0), pl.program_id(1), pl.program_id(2)
    nf = pl.num_programs(2)
    tm = hn_ref.shape[0]
    ch, sub = MOE_CHUNK, MOE_SUB
    nsub = tm // sub
    cum = [cum_ref[(s * (nsub + 1) + r) * N_EXPERTS + e] for r in range(nsub + 1)]
    nch = (cum[nsub] + (ch - 1)) // ch

    @pl.when((e == 0) & (f == 0))
    def _():
        o_ref[...] = jnp.zeros_like(o_ref)

    @pl.when(f == 0)
    def _():
        def clear(c, carry):
            r0 = pl.multiple_of(c * ch, ch)
            xc_ref[pl.ds(r0, ch), :] = jnp.zeros((ch, xc_ref.shape[1]), BF16)
            y_ref[pl.ds(r0, ch), :] = jnp.zeros((ch, y_ref.shape[1]), F32)
            return carry
        lax.fori_loop(0, nch + 1, clear, 0)

        slot = lax.broadcasted_iota(I32, (ch, sub), 0).astype(F32)
        for r in range(nsub):
            prow = post_ref[pl.ds(e, 1), r * sub:(r + 1) * sub]

            def gather(c, carry, r=r, prow=prow):
                r0 = pl.multiple_of(c * ch, ch)
                sel = jnp.where(prow - (c * ch).astype(F32) == slot, 1.0, 0.0).astype(BF16)
                rows = _dot(sel, hn_ref[r * sub:(r + 1) * sub, :])
                xc_ref[pl.ds(r0, ch), :] = (xc_ref[pl.ds(r0, ch), :].astype(F32) + rows).astype(BF16)
                return carry
            lax.fori_loop(cum[r] // ch, (cum[r + 1] + (ch - 1)) // ch, gather, 0)

    def expert(c, carry):
        r0 = pl.multiple_of(c * ch, ch)
        xc = xc_ref[pl.ds(r0, ch), :]
        a = _silu(_dot(xc, w1_ref[...])) * _dot(xc, w3_ref[...])
        y_ref[pl.ds(r0, ch), :] += _dot(a.astype(BF16), w2_ref[...])
        return carry
    lax.fori_loop(0, nch, expert, 0)

    @pl.when(f == nf - 1)
    def _():
        lane = lax.broadcasted_iota(I32, (tm, LANES), 1)
        pcol = jnp.sum(jnp.where(lane == e, pos_ref[...], 0.0), axis=-1, keepdims=True)
        gcol = jnp.sum(jnp.where(lane == e, comb_ref[...], 0.0), axis=-1, keepdims=True)
        slot = lax.broadcasted_iota(I32, (sub, 2 * ch), 1).astype(F32)
        for r in range(nsub):
            rows = slice(r * sub, (r + 1) * sub)

            def scatter(c2, carry, rows=rows):
                r0 = pl.multiple_of(c2 * (2 * ch), 2 * ch)
                z = y_ref[pl.ds(r0, 2 * ch), :].astype(BF16)
                sel = jnp.where(pcol[rows] - (c2 * (2 * ch)).astype(F32) == slot, 1.0, 0.0).astype(BF16)
                o_ref[rows, :] += gcol[rows] * _dot(sel, z)
                return carry
            lax.fori_loop(cum[r] // (2 * ch), (cum[r + 1] + (2 * ch - 1)) // (2 * ch), scatter, 0)


def _moe(hn, comb, pos, post, cum, w1, w3, w2):
    n, d = hn.shape
    ne, _, dff = w1.shape
    ns, _, tm = post.shape
    tf = dff // 4
    cap = tm + 2 * MOE_CHUNK
    grid_spec = pltpu.PrefetchScalarGridSpec(
        num_scalar_prefetch=1,
        grid=(ns, ne, dff // tf),
        in_specs=[
            pl.BlockSpec((tm, d), lambda i, e, f, c: (i, 0)),
            pl.BlockSpec((tm, LANES), lambda i, e, f, c: (i, 0)),
            pl.BlockSpec((tm, LANES), lambda i, e, f, c: (i, 0)),
            pl.BlockSpec((None, N_EXPERTS, tm), lambda i, e, f, c: (i, 0, 0)),
            pl.BlockSpec((None, d, tf), lambda i, e, f, c: (e, 0, f)),
            pl.BlockSpec((None, d, tf), lambda i, e, f, c: (e, 0, f)),
            pl.BlockSpec((None, tf, d), lambda i, e, f, c: (e, f, 0)),
        ],
        out_specs=pl.BlockSpec((tm, d), lambda i, e, f, c: (i, 0)),
        scratch_shapes=[pltpu.VMEM((cap, d), BF16), pltpu.VMEM((cap, d), F32)],
    )
    return pl.pallas_call(
        _moe_kernel,
        grid_spec=grid_spec,
        out_shape=jax.ShapeDtypeStruct((n, d), F32),
        compiler_params=_cparams(("arbitrary", "arbitrary", "arbitrary")),
        name="ffn_experts",
    )(cum, hn, comb, pos, post, w1, w3, w2)


def _residual_kernel(x_ref, f_ref, gate_ref, gn_ref, o_ref, *, final):
    y = x_ref[...] + gate_ref[...] * f_ref[...]
    o_ref[...] = _rms(y, gn_ref[...]) if final else y


def _residual(x2, f2, mod3, gn, s_len, final):
    n, d = x2.shape
    tm = TM_FFN
    tpb = s_len // tm
    row = pl.BlockSpec((tm, d), lambda i: (i, 0))
    return pl.pallas_call(
        functools.partial(_residual_kernel, final=final),
        grid=(n // tm,),
        in_specs=[row, row, pl.BlockSpec((None, 1, d), lambda i: (i // tpb, 0, 5)),
                  pl.BlockSpec((1, d), lambda i: (0, 0))],
        out_specs=row,
        out_shape=jax.ShapeDtypeStruct((n, d), F32),
        compiler_params=_cparams(("arbitrary",)),
        name="residual",
    )(x2, f2, mod3, gn)


def _final_kernel(x_ref, g_ref, o_ref):
    o_ref[...] = _rms(x_ref[...], g_ref[...])


def _final_norm(x2, g):
    n, d = x2.shape
    tm = TM_FFN
    return pl.pallas_call(
        _final_kernel,
        grid=(n // tm,),
        in_specs=[pl.BlockSpec((tm, d), lambda i: (i, 0)), pl.BlockSpec((1, d), lambda i: (0, 0))],
        out_specs=pl.BlockSpec((tm, d), lambda i: (i, 0)),
        out_shape=jax.ShapeDtypeStruct((n, d), F32),
        compiler_params=_cparams(("arbitrary",)),
        name="final_norm",
    )(x2, g)


def _alibi_slopes():
    return 2.0 ** (-8.0 * jnp.arange(1, N_HEADS + 1, dtype=F32) / N_HEADS)


def _block_tables():
    j = jnp.arange(TQ)[:, None]
    i = jnp.arange(TQ)[None, :]
    diagm = jnp.where((j // CHUNK) <= (i // CHUNK), 0.0, NEG).astype(F32)
    corr = -2.0 * _alibi_slopes()[:, None, None] * jnp.maximum(j - i, 0).astype(F32)[None]
    return corr, diagm


def _rope_tables(s_len):
    inv = ROPE_THETA ** (-jnp.arange(0, QK_ROPE, 2, dtype=F32) / QK_ROPE)
    ang = jnp.arange(s_len, dtype=F32)[:, None] * inv[None, :]
    cos2 = jnp.concatenate([jnp.cos(ang)] * 2, axis=-1)
    sin2 = jnp.concatenate([jnp.sin(ang)] * 2, axis=-1)
    z = jnp.zeros((s_len, QK_NOPE), F32)
    z2 = jnp.zeros((s_len, LANES - QK_NOPE - QK_ROPE), F32)
    return (jnp.concatenate([z, cos2, z2], axis=-1), jnp.concatenate([z, sin2, z2], axis=-1))


def _rot_cols(w):
    half = QK_ROPE // 2
    return jnp.concatenate([-w[..., half:], w[..., :half]], axis=-1)


def _slabs(a, nh, extra=None):
    b, s_len, _ = a.shape
    parts = [a.reshape(b, s_len, nh, -1)]
    if extra is not None:
        parts.append(jnp.broadcast_to(extra, (b, s_len, nh, extra.shape[-1])).astype(a.dtype))
    used = sum(p.shape[-1] for p in parts)
    parts.append(jnp.zeros((b, s_len, nh, LANES - used), a.dtype))
    return jnp.concatenate(parts, axis=-1)


def _rows(slab):
    b, s_len = slab.shape[:2]
    return slab.reshape(b, s_len, -1)


def _cols(slab):
    b, s_len = slab.shape[:2]
    return jnp.transpose(slab, (0, 2, 3, 1)).reshape(b, -1, s_len)


def _alibi_extras(s_len, nslab):
    loc = (jnp.arange(s_len) % TQ).astype(F32)[:, None, None]
    sl = jnp.repeat(_alibi_slopes(), nslab // N_HEADS)[None, :, None]
    q_extra = jnp.concatenate([-sl * loc, jnp.broadcast_to(sl, (s_len, nslab, 1))], axis=-1)
    k_extra = jnp.concatenate([jnp.ones((s_len, nslab, 1), F32),
                               jnp.broadcast_to(loc, (s_len, nslab, 1))], axis=-1)
    return q_extra[None], k_extra[None]


def _layer_weights(l, w_in, mla_w_uq, mla_w_ukv):
    w = w_in[l]
    names = ('a_q', 'a_k', 'a_v', 'b_qd', 'b_kvd', 'b_kr', 'c_q', 'c_k', 'c_v', 'c_qi', 'c_ki',
             'c_wi', 'd_q', 'd_k', 'd_v')
    widths = (256, 256, 256, Q_LORA, KV_LORA, QK_ROPE, 256, 256, 256, IDX_HEADS * IDX_DIM, IDX_DIM,
              IDX_HEADS, 256, 256, 256)
    cols, o = {}, 0
    for nme, wd in zip(names, widths):
        cols[nme] = w[:, o:o + wd]
        o += wd
    pad = jnp.zeros((w.shape[0], N_SIDE - (Q_LORA + KV_LORA + 2 * QK_ROPE + IDX_DIM + IDX_HEADS)), F32)
    wcat = jnp.concatenate(
        [cols['a_q'], cols['a_k'], cols['a_v'], cols['c_q'], cols['c_k'], cols['c_v'], cols['c_qi'],
         cols['d_q'], cols['d_k'], cols['d_v'],
         cols['b_qd'], cols['b_kvd'], cols['b_kr'], _rot_cols(cols['b_kr']), cols['c_ki'],
         cols['c_wi'], pad], axis=-1).astype(BF16)
    ones = lambda k: jnp.ones((k,), F32)
    cs = jnp.concatenate([
        ones(256) * DIFF_D ** -0.5, ones(512),
        ones(256) * HEAD_DIM ** -0.5, ones(768),
        ones(256) * HEAD_DIM ** -0.5, ones(512), ones(N_SIDE)])[None, :]

    uq = mla_w_uq[l].reshape(Q_LORA, N_HEADS, QK_NOPE + QK_ROPE)
    zq = jnp.zeros((Q_LORA, N_HEADS, LANES - QK_NOPE - QK_ROPE), F32)
    wqa = jnp.concatenate([uq, zq], axis=-1).reshape(Q_LORA, N_HEADS * LANES).astype(BF16)
    wqb = jnp.concatenate([jnp.zeros((Q_LORA, N_HEADS, QK_NOPE), F32), _rot_cols(uq[..., QK_NOPE:]), zq],
                          axis=-1).reshape(Q_LORA, N_HEADS * LANES).astype(BF16)
    ukv = mla_w_ukv[l].reshape(KV_LORA, N_HEADS, QK_NOPE + HEAD_DIM)
    zk = jnp.zeros((KV_LORA, N_HEADS, LANES - QK_NOPE), F32)
    wk = jnp.concatenate([ukv[..., :QK_NOPE], zk], axis=-1).reshape(KV_LORA, N_HEADS * LANES).astype(BF16)
    wv = jnp.concatenate([ukv[..., QK_NOPE:], zk], axis=-1).reshape(KV_LORA, N_HEADS * LANES).astype(BF16)
    return wcat, cs, wqa, wqb, wk, wv


def _placement():
    r = jnp.arange(LANES)[:, None]
    c = jnp.arange(N_HEADS * LANES)[None, :] % LANES
    e1 = ((c >= QK_NOPE) & (c < QK_NOPE + QK_ROPE) & (r == c - QK_NOPE)).astype(BF16)
    e2 = ((c >= QK_NOPE) & (c < QK_NOPE + QK_ROPE) & (r == c - QK_NOPE + QK_ROPE)).astype(BF16)
    return e1, e2


def kernel(x, c, ada_w, ada_b, mix_norm_g, ffn_norm_g, w_in, w_out, diff_lambda, diff_norm_g, mla_q_norm_g, mla_kv_norm_g, mla_w_uq, mla_w_ukv, band_rel_bias, ffn_w1, ffn_w3, ffn_w2, moe_router, moe_w1, moe_w3, moe_w2, final_norm_g):
    b, s_len, d = x.shape
    depth = ada_w.shape[0]
    n = b * s_len
    n_sel = min(TOPK_MAX, s_len // 4)
    assert d == D_MODEL and s_len % TM_FFN == 0 and s_len % TQ == 0

    mod = _modulation(c, ada_w, ada_b)
    corr, diagm = _block_tables()
    slopes = _alibi_slopes()
    cos_t, sin_t = _rope_tables(s_len)
    e1, e2 = _placement()
    tril = (jnp.arange(TQ)[None, :] <= jnp.arange(TQ)[:, None]).astype(BF16)
    mla_scale = (QK_NOPE + QK_ROPE) ** -0.5
    qx8, kx8 = _alibi_extras(s_len, 2 * N_HEADS)
    qx4, kx4 = _alibi_extras(s_len, N_HEADS)
    one4 = jnp.ones((1, s_len, N_HEADS, 1), F32)
    tr = lambda a: jnp.swapaxes(a, 1, 2)
    front = 2 * TQ

    x2 = x.reshape(n, d)
    for l in range(depth):
        mod3 = mod[l].reshape(b, 1, 6 * d)
        wcat, cs, wqa, wqb, wk, wv = _layer_weights(l, w_in, mla_w_uq, mla_w_ukv)
        main, side = _inproj(x2, mix_norm_g[l][None], mod3, wcat, cs, s_len)
        main3 = main.reshape(b, s_len, N_MAIN)
        side3 = side.reshape(b, s_len, N_SIDE)
        blk = lambda k: main3[:, :, k * 256:(k + 1) * 256]

        lam_init = 0.8 - 0.6 * math.exp(-0.3 * l)
        o_a = _attn_a(_cols(_slabs(blk(0), 2 * N_HEADS, qx8)), _rows(_slabs(blk(1), 2 * N_HEADS, kx8)),
                      _cols(_slabs(blk(2), N_HEADS, one4)), corr + diagm[None], slopes, diff_lambda[l],
                      jnp.broadcast_to(diff_norm_g[l][:, None], (2 * DIFF_D, TQ)), lam_init)

        qb_, kb_, vb_ = _mla_prep(side, mla_q_norm_g[l][None], mla_kv_norm_g[l][None], wqa, wqb, wk, wv,
                                  e1, e2, cos_t, sin_t, s_len, mla_scale)
        o_b = _attn_b(tr(qb_.reshape(b, s_len, -1)), kb_.reshape(b, s_len, -1),
                      tr(vb_.reshape(b, s_len, -1)), diagm)

        o0 = Q_LORA + KV_LORA + 2 * QK_ROPE
        ki = side3[:, :, o0:o0 + IDX_DIM].astype(BF16)
        wt = tr(side3[:, :, o0 + IDX_DIM:o0 + IDX_DIM + IDX_HEADS])
        o_c = _attn_c(_cols(_slabs(blk(3), N_HEADS, qx4)), tr(blk(6)), wt, _rows(_slabs(blk(4), N_HEADS, kx4)),
                      ki, _cols(_slabs(blk(5), N_HEADS, one4)), corr, diagm, slopes, tril, n_sel)

        k_pad = jnp.pad(_rows(_slabs(blk(8), N_HEADS)), ((0, 0), (front, 0), (0, 0)))
        vt_pad = jnp.pad(_cols(_slabs(blk(9), N_HEADS, one4)), ((0, 0), (0, 0), (front, 0)))
        o_d = _attn_d(_cols(_slabs(blk(7), N_HEADS)), k_pad, vt_pad, _band_bias(band_rel_bias[l]))

        flat = lambda a: a.reshape(n, 256)
        x2 = _outproj(x2, flat(o_a), flat(o_b), flat(o_c), flat(o_d), w_out[l].astype(BF16), mod3, s_len)

        gf = ffn_norm_g[l][None]
        if l % 2 == 0:
            i = l // 2
            x2 = _ffn(x2, gf, mod3, ffn_w1[i].astype(BF16), ffn_w3[i].astype(BF16),
                      ffn_w2[i].astype(BF16), s_len)
        else:
            i = l // 2
            r = jnp.pad(moe_router[i], ((0, 0), (0, LANES - N_EXPERTS)))
            rhi = r.astype(BF16)
            rlo = (r - rhi.astype(F32)).astype(BF16)
            tril_sub = (jnp.arange(MOE_SUB)[None, :] <= jnp.arange(MOE_SUB)[:, None]).astype(BF16)
            hn, comb, pos, post, cum = _router(x2, gf, mod3, rhi, rlo, tril_sub, s_len)
            nsub = post.shape[-1] // MOE_SUB
            cum_i = cum[:, :nsub + 1, :N_EXPERTS].astype(I32).reshape(-1)
            f2 = _moe(hn, comb, pos, post, cum_i, moe_w1[i].astype(BF16), moe_w3[i].astype(BF16),
                      moe_w2[i].astype(BF16))
            last = l == depth - 1
            x2 = _residual(x2, f2, mod3, final_norm_g[None], s_len, last)
            if last:
                return x2.reshape(b, s_len, d)
    return _final_norm(x2, final_norm_g[None]).reshape(b, s_len, d)
```

```python
import functools
import math

import numpy as np
import jax
import jax.numpy as jnp
from jax import lax
from jax.experimental import pallas as pl
from jax.experimental.pallas import tpu as pltpu

F32 = jnp.float32
BF16 = jnp.bfloat16
I32 = jnp.int32
I16 = jnp.int16

D_MODEL = 1024
CHUNK = 64
N_HEADS = 4
HEAD_DIM = 64
DIFF_D = 32
Q_LORA = 256
KV_LORA = 128
QK_NOPE = 64
QK_ROPE = 32
ROPE_THETA = 10000.0
IDX_HEADS = 8
IDX_DIM = 32
TOPK_MAX = 256
BAND_CHUNKS = 8
REL_CLIP = 128
N_EXPERTS = 8
EPS = 1e-6

NEG = -1e30
INT_MIN = -(2 ** 31)
I16_MIN = -(2 ** 15)

LANES = 128
PACK16 = 16
VMEM_LIMIT = 56 * 1024 * 1024
TQ = 256
SWEEP_UNITS = 16
TM_PROJ = 512
TM_FFN = 512
TM_MOE = 2048
MOE_CHUNK = 128
MOE_SUB = 512
N_MAIN = 2560
N_SIDE = 512


def _cparams(sem):
    return pltpu.CompilerParams(dimension_semantics=sem, vmem_limit_bytes=VMEM_LIMIT)


def _dot(a, b):
    return jnp.dot(a, b, preferred_element_type=F32)


def _silu(x):
    return x / (1.0 + jnp.exp(-x))


def _rms(x, g):
    return x * lax.rsqrt(jnp.mean(x * x, axis=-1, keepdims=True) + EPS) * g


def _adaln(x, g, shift, scale):
    return _rms(x, g) * (1.0 + scale) + shift


def _mod_kernel(c_ref, w_ref, b_ref, o_ref):
    sc = _silu(c_ref[...]).astype(BF16)
    o_ref[...] = _dot(sc, w_ref[...].astype(BF16)) + b_ref[...]


def _modulation(c, ada_w, ada_b):
    depth, d, n6 = ada_w.shape
    b = c.shape[0]
    tn = 1536
    return pl.pallas_call(
        _mod_kernel,
        grid=(depth, n6 // tn),
        in_specs=[
            pl.BlockSpec((b, d), lambda l, j: (0, 0)),
            pl.BlockSpec((None, d, tn), lambda l, j: (l, 0, j)),
            pl.BlockSpec((None, 1, tn), lambda l, j: (l, 0, j)),
        ],
        out_specs=pl.BlockSpec((None, b, tn), lambda l, j: (l, 0, j)),
        out_shape=jax.ShapeDtypeStruct((depth, b, n6), F32),
        compiler_params=_cparams(("arbitrary", "arbitrary")),
        name="modulation",
    )(c, ada_w, ada_b.reshape(depth, 1, n6))


def _inproj_kernel(x_ref, g_ref, sh_ref, sc_ref, w_ref, cs_ref, main_ref, side_ref):
    hn = _adaln(x_ref[...], g_ref[...], sh_ref[...], sc_ref[...]).astype(BF16)
    step = 512
    for c0 in range(0, N_MAIN + N_SIDE, step):
        acc = _dot(hn, w_ref[:, c0:c0 + step]) * cs_ref[:, c0:c0 + step]
        if c0 < N_MAIN:
            main_ref[:, c0:c0 + step] = acc.astype(BF16)
        else:
            side_ref[:, c0 - N_MAIN:c0 - N_MAIN + step] = acc


def _inproj(x2, g, mod3, w, cs, s_len):
    n, d = x2.shape
    tm = TM_PROJ
    tpb = s_len // tm
    nc = N_MAIN + N_SIDE
    return pl.pallas_call(
        _inproj_kernel,
        grid=(n // tm,),
        in_specs=[
            pl.BlockSpec((tm, d), lambda i: (i, 0)),
            pl.BlockSpec((1, d), lambda i: (0, 0)),
            pl.BlockSpec((None, 1, d), lambda i: (i // tpb, 0, 0)),
            pl.BlockSpec((None, 1, d), lambda i: (i // tpb, 0, 1)),
            pl.BlockSpec((d, nc), lambda i: (0, 0)),
            pl.BlockSpec((1, nc), lambda i: (0, 0)),
        ],
        out_specs=[
            pl.BlockSpec((tm, N_MAIN), lambda i: (i, 0)),
            pl.BlockSpec((tm, N_SIDE), lambda i: (i, 0)),
        ],
        out_shape=[
            jax.ShapeDtypeStruct((n, N_MAIN), BF16),
            jax.ShapeDtypeStruct((n, N_SIDE), F32),
        ],
        compiler_params=_cparams(("arbitrary",)),
        name="inproj",
    )(x2, g, mod3, mod3, w, cs)


def _dot_t(a, b):
    return lax.dot_general(a, b, (((0,), (0,)), ((), ())), preferred_element_type=F32)


def _softmax_update(s, c, m, acc_ref, slot, v):
    m_new = jnp.maximum(m, jnp.max(s, axis=0, keepdims=True) + c)
    alpha = jnp.exp(m - m_new)
    p = jnp.exp(s - (m_new - c))
    acc_ref[slot] = acc_ref[slot] * alpha + _dot_t(v, p.astype(BF16))
    return m_new


def _normalized(acc_ref, slot):
    a = acc_ref[slot]
    return a[:HEAD_DIM] / a[HEAD_DIM:HEAD_DIM + 1]


def _pipelined(n, produce, consume, lookahead=3):
    vals = [produce(i) for i in range(min(lookahead, n))]
    for i in range(n):
        if i + lookahead < n:
            vals.append(produce(i + lookahead))
        consume(i, vals[i])
        vals[i] = None


def _load_qt(q_ref, qt_ref, width, qaug_ref=None, slabs_per_aug=1):
    tq = q_ref.shape[0]
    qt = jnp.transpose(q_ref[...].astype(F32))
    for i in range(qt.shape[0] // width):
        parts = [qt[i * width:(i + 1) * width]]
        if qaug_ref is not None:
            parts.append(qaug_ref[i // slabs_per_aug])
        used = sum(p.shape[0] for p in parts)
        if used < LANES:
            parts.append(jnp.zeros((LANES - used, tq), F32))
        qt_ref[i * LANES:(i + 1) * LANES, :] = jnp.concatenate(parts, axis=0).astype(BF16)


def _sweep(qb, body, carry, group):
    ng = qb // group
    carry = lax.fori_loop(0, ng, lambda g, c: body(g * group, c, group), carry)
    return lax.fori_loop(ng * group, qb, lambda kb, c: body(kb, c, 1), carry)


def _block_const(slope_ref, h, qb, kb, tq):
    return -slope_ref[h] * ((qb - kb) * tq).astype(F32)


def _attn_a_kernel(q_ref, qaug_ref, k_ref, vt_ref, corr_ref, slope_ref, lamv_ref, g_ref, o_ref, acc_ref,
                   qt_ref, *, lam_init):
    qb = pl.program_id(1)
    tq = o_ref.shape[0]
    nhm = 2 * N_HEADS
    acc_ref[...] = jnp.zeros_like(acc_ref)
    _load_qt(q_ref, qt_ref, DIFF_D, qaug_ref, slabs_per_aug=2)

    def block(kb, ms, nb, diag=False):
        out = list(ms)
        start = lambda u: pl.multiple_of((kb + u // nhm) * tq, tq)

        def logits(u):
            i = u % nhm
            return _dot(k_ref[pl.ds(start(u), tq), i * LANES:(i + 1) * LANES], qt_ref[i * LANES:(i + 1) * LANES, :])

        def update(u, s):
            i = u % nhm
            h = i // 2
            vt = vt_ref[pl.ds(start(u), tq), h * LANES:(h + 1) * LANES]
            if diag:
                s, c = corr_ref[h] + s, 0.0
            else:
                c = _block_const(slope_ref, h, qb, kb + u // nhm, tq)
            out[i] = _softmax_update(s, c, out[i], acc_ref, i, vt)

        _pipelined(nb * nhm, logits, update)
        return tuple(out)

    ms = tuple(jnp.full((1, tq), NEG, F32) for _ in range(nhm))
    ms = block(qb, ms, 1, diag=True)
    _sweep(qb, block, ms, SWEEP_UNITS // nhm)

    lv = lamv_ref[...]
    lam = (jnp.exp(jnp.sum(lv[0:1] * lv[1:2], axis=-1, keepdims=True))
           - jnp.exp(jnp.sum(lv[2:3] * lv[3:4], axis=-1, keepdims=True)) + lam_init)
    outs = []
    for h in range(N_HEADS):
        o = _normalized(acc_ref, 2 * h) - lam * _normalized(acc_ref, 2 * h + 1)
        o = o * lax.rsqrt(jnp.mean(o * o, axis=0, keepdims=True) + EPS) * g_ref[...]
        outs.append(o * (1.0 - lam_init))
    o_ref[...] = jnp.transpose(jnp.concatenate(outs, axis=0)).astype(o_ref.dtype)


def _attn_a(main3, qaug, k, vt, corr, slopes, lamv, gcol, lam_init):
    b, s_len, _ = k.shape
    nq = s_len // TQ
    return pl.pallas_call(
        functools.partial(_attn_a_kernel, lam_init=lam_init),
        grid=(b, nq),
        in_specs=[
            pl.BlockSpec((None, TQ, 256), lambda i, j: (i, j, 0)),
            pl.BlockSpec(qaug.shape, lambda i, j: (0, 0, 0)),
            pl.BlockSpec((None, s_len, 2 * N_HEADS * LANES), lambda i, j: (i, 0, 0)),
            pl.BlockSpec((None, s_len, N_HEADS * LANES), lambda i, j: (i, 0, 0)),
            pl.BlockSpec((N_HEADS, TQ, TQ), lambda i, j: (0, 0, 0)),
            pl.BlockSpec(memory_space=pltpu.SMEM),
            pl.BlockSpec((4, DIFF_D), lambda i, j: (0, 0)),
            pl.BlockSpec((2 * DIFF_D, TQ), lambda i, j: (0, 0)),
        ],
        out_specs=pl.BlockSpec((None, TQ, 256), lambda i, j: (i, j, 0)),
        out_shape=jax.ShapeDtypeStruct((b, s_len, 256), BF16),
        scratch_shapes=[pltpu.VMEM((2 * N_HEADS, LANES, TQ), F32),
                        pltpu.VMEM((2 * N_HEADS * LANES, TQ), BF16)],
        compiler_params=_cparams(("arbitrary", "arbitrary")),
        name="attn_diff",
    )(main3, qaug, k, vt, corr, slopes, lamv, gcol)


def _mla_prep_kernel(side_ref, gq_ref, gkv_ref, wqa_ref, wqb_ref, wk_ref, wv_ref, e1_ref, e2_ref,
                     cos_ref, sin_ref, q_ref, k_ref, v_ref, *, qscale):
    side = side_ref[...]
    hq = _rms(side[:, :Q_LORA], gq_ref[...]).astype(BF16)
    hkv = _rms(side[:, Q_LORA:Q_LORA + KV_LORA], gkv_ref[...]).astype(BF16)
    misc = side[:, Q_LORA + KV_LORA:]
    cos4 = jnp.concatenate([cos_ref[...]] * N_HEADS, axis=-1)
    sin4 = jnp.concatenate([sin_ref[...]] * N_HEADS, axis=-1)
    lane = lax.broadcasted_iota(I32, (1, N_HEADS * LANES), 1) % LANES
    nope = (lane < QK_NOPE).astype(F32)
    ones_col = (lane == HEAD_DIM).astype(F32)
    q = (_dot(hq, wqa_ref[...]) * (nope + cos4) + _dot(hq, wqb_ref[...]) * sin4) * qscale
    q_ref[...] = q.astype(BF16)
    hi = misc.astype(BF16)
    lo = (misc - hi.astype(F32)).astype(BF16)
    kr = _dot(hi, e1_ref[...]) + _dot(lo, e1_ref[...])
    krp = _dot(hi, e2_ref[...]) + _dot(lo, e2_ref[...])
    k = _dot(hkv, wk_ref[...]) + kr * cos4 + krp * sin4
    k_ref[...] = k.astype(BF16)
    v_ref[...] = (_dot(hkv, wv_ref[...]) + ones_col).astype(BF16)


def _mla_prep(side, gq, gkv, wqa, wqb, wk, wv, e1, e2, cos_t, sin_t, s_len, qscale):
    n = side.shape[0]
    tm = TM_PROJ
    tpb = s_len // tm
    full = lambda a: pl.BlockSpec(a.shape, lambda i: (0,) * a.ndim)
    out = jax.ShapeDtypeStruct((n, N_HEADS * LANES), BF16)
    return pl.pallas_call(
        functools.partial(_mla_prep_kernel, qscale=qscale),
        grid=(n // tm,),
        in_specs=[pl.BlockSpec((tm, N_SIDE), lambda i: (i, 0)),
                  full(gq), full(gkv), full(wqa), full(wqb), full(wk), full(wv), full(e1), full(e2),
                  pl.BlockSpec((tm, LANES), lambda i: (i % tpb, 0)),
                  pl.BlockSpec((tm, LANES), lambda i: (i % tpb, 0))],
        out_specs=[pl.BlockSpec((tm, N_HEADS * LANES), lambda i: (i, 0))] * 3,
        out_shape=[out, out, out],
        compiler_params=_cparams(("arbitrary",)),
        name="mla_prep",
    )(side, gq, gkv, wqa, wqb, wk, wv, e1, e2, cos_t, sin_t)


def _attn_b_kernel(q_ref, k_ref, vt_ref, diagm_ref, o_ref, acc_ref, qt_ref):
    qb = pl.program_id(1)
    tq = o_ref.shape[0]
    acc_ref[...] = jnp.zeros_like(acc_ref)
    _load_qt(q_ref, qt_ref, LANES)

    def block(kb, ms, nb, diag=False):
        out = list(ms)
        start = lambda u: pl.multiple_of((kb + u // N_HEADS) * tq, tq)

        def logits(u):
            h = u % N_HEADS
            return _dot(k_ref[pl.ds(start(u), tq), h * LANES:(h + 1) * LANES], qt_ref[h * LANES:(h + 1) * LANES, :])

        def update(u, s):
            h = u % N_HEADS
            vt = vt_ref[pl.ds(start(u), tq), h * LANES:(h + 1) * LANES]
            if diag:
                s = diagm_ref[...] + s
            out[h] = _softmax_update(s, 0.0, out[h], acc_ref, h, vt)

        _pipelined(nb * N_HEADS, logits, update)
        return tuple(out)

    ms = tuple(jnp.full((1, tq), NEG, F32) for _ in range(N_HEADS))
    ms = block(qb, ms, 1, diag=True)
    _sweep(qb, block, ms, SWEEP_UNITS // N_HEADS)
    o_ref[...] = jnp.transpose(jnp.concatenate([_normalized(acc_ref, h) for h in range(N_HEADS)],
                                               axis=0)).astype(o_ref.dtype)


def _attn_b(q3, k, vt, diagm):
    b, s_len, _ = k.shape
    nq = s_len // TQ
    return pl.pallas_call(
        _attn_b_kernel,
        grid=(b, nq),
        in_specs=[
            pl.BlockSpec((None, TQ, N_HEADS * LANES), lambda i, j: (i, j, 0)),
            pl.BlockSpec((None, s_len, N_HEADS * LANES), lambda i, j: (i, 0, 0)),
            pl.BlockSpec((None, s_len, N_HEADS * LANES), lambda i, j: (i, 0, 0)),
            pl.BlockSpec((TQ, TQ), lambda i, j: (0, 0)),
        ],
        out_specs=pl.BlockSpec((None, TQ, 256), lambda i, j: (i, j, 0)),
        out_shape=jax.ShapeDtypeStruct((b, s_len, 256), BF16),
        scratch_shapes=[pltpu.VMEM((N_HEADS, LANES, TQ), F32),
                        pltpu.VMEM((N_HEADS * LANES, TQ), BF16)],
        compiler_params=_cparams(("arbitrary", "arbitrary")),
        name="attn_latent",
    )(q3, k, vt, diagm)


def _attn_c_kernel(q_ref, qi_ref, qaug_ref, wt_ref, k_ref, ki_ref, vt_ref, corr_ref, diagm_ref, slope_ref,
                   tril_ref, o_ref, keys_ref, khi_ref, klo_ref, acc_ref, qt_ref, qit_ref, *, n_sel):
    qb = pl.program_id(1)
    tq = o_ref.shape[0]
    nkb = qb + 1
    _load_qt(q_ref, qt_ref, HEAD_DIM, qaug_ref)
    qit_ref[...] = jnp.transpose(qi_ref[...].astype(F32)).astype(BF16)

    wt = wt_ref[...] * (IDX_HEADS ** -0.5 * IDX_DIM ** -0.5)

    def score_block(kb, diag):
        k0 = pl.multiple_of(kb * tq, tq)
        ki = ki_ref[pl.ds(k0, tq), :]
        total = [jnp.zeros((tq, tq), F32)]

        def weighted(h, d):
            total[0] = total[0] + jnp.maximum(d, 0.0) * wt[h:h + 1]

        _pipelined(IDX_HEADS, lambda h: _dot(ki, qit_ref[h * IDX_DIM:(h + 1) * IDX_DIM, :]), weighted)
        sc = jnp.where(total[0] == 0.0, 0.0, total[0])
        bits = lax.bitcast_convert_type(sc, I32)
        key = bits ^ ((bits >> 31) & 0x7FFFFFFF)
        if diag:
            key = jnp.where(diagm_ref[...] < 0.0, INT_MIN, key)
        keys_ref[pl.ds(k0, tq), :] = key
        khi_ref[pl.ds(k0, tq), :] = (key >> 16).astype(I16)
        klo_ref[pl.ds(k0, tq), :] = ((key & 0xFFFF) - 32768).astype(I16)

    score_block(qb, True)

    def _score_loop(kb, c):
        score_block(kb, False)
        return c
    lax.fori_loop(0, qb, _score_loop, 0)

    def count_ge(cand):
        def body(kb, cnt8):
            k0 = pl.multiple_of(kb * tq, tq)
            hit = jnp.where(keys_ref[pl.ds(k0, tq), :] >= cand, 1, 0)
            return cnt8 + jnp.sum(hit.reshape(tq // 8, 8, tq), axis=0)
        cnt8 = lax.fori_loop(0, nkb, body, jnp.zeros((8, tq), I32))
        return jnp.sum(cnt8, axis=0, keepdims=True)

    def count16(ref, cand):
        cand16 = cand.astype(I16)

        def body(kb, cnt):
            k0 = pl.multiple_of(kb * tq, tq)
            hit = jnp.where(ref[pl.ds(k0, tq), :] >= cand16, jnp.int16(1), jnp.int16(0))
            for j in range(tq // PACK16):
                cnt = cnt + hit[j * PACK16:(j + 1) * PACK16]
            return cnt
        cnt = lax.fori_loop(0, nkb, body, jnp.zeros((PACK16, tq), I16))
        return jnp.sum(cnt.astype(I32), axis=0, keepdims=True)

    def select16(ref, k):
        zero = jnp.zeros((1, tq), I32)
        t = jnp.where(count16(ref, zero) >= k, zero, I16_MIN)

        def bit_body(i, t):
            cand = t + (jnp.int32(1) << (14 - i))
            return jnp.where(count16(ref, cand) >= k, cand, t)
        return lax.fori_loop(0, 15, bit_body, t)

    thr_hi = select16(khi_ref, n_sel)
    above = jnp.where(thr_hi >= 32767, 0, count16(khi_ref, thr_hi + 1))
    thr_hi16 = thr_hi.astype(I16)

    def _bucket(kb, c):
        k0 = pl.multiple_of(kb * tq, tq)
        klo_ref[pl.ds(k0, tq), :] = jnp.where(khi_ref[pl.ds(k0, tq), :] == thr_hi16,
                                              klo_ref[pl.ds(k0, tq), :], jnp.int16(I16_MIN))
        return c
    lax.fori_loop(0, nkb, _bucket, 0)
    thr_lo = select16(klo_ref, n_sel - above)
    thr = thr_hi * 65536 + (thr_lo + 32768)
    thr = jnp.maximum(thr, INT_MIN + 1)
    c_gt = count_ge(thr + 1)
    c_ge = count_ge(thr)
    need = (n_sel - c_gt).astype(F32)
    has_ties = jnp.max(c_ge) > n_sel

    def attend(kb, carry, nb, diag=False, ties=False):
        ms, eq_before = carry
        out = list(ms)
        start = lambda j: pl.multiple_of((kb + j) * tq, tq)
        sels = []
        for j in range(nb):
            key = keys_ref[pl.ds(start(j), tq), :]
            if ties:
                eq = jnp.where(key == thr, 1.0, 0.0)
                rank = _dot(tril_ref[...], eq.astype(BF16)) + eq_before
                sels.append((key > thr) | ((key == thr) & (rank <= need)))
                eq_before = eq_before + jnp.sum(eq, axis=0, keepdims=True)
            else:
                sels.append(key >= thr)

        def logits(u):
            h, j = u % N_HEADS, u // N_HEADS
            return _dot(k_ref[pl.ds(start(j), tq), h * LANES:(h + 1) * LANES], qt_ref[h * LANES:(h + 1) * LANES, :])

        def update(u, s):
            h, j = u % N_HEADS, u // N_HEADS
            vt = vt_ref[pl.ds(start(j), tq), h * LANES:(h + 1) * LANES]
            if diag:
                s, c = corr_ref[h] + s, 0.0
            else:
                c = _block_const(slope_ref, h, qb, kb + j, tq)
            s = jnp.where(sels[j], s, NEG)
            out[h] = _softmax_update(s, c, out[h], acc_ref, h, vt)

        _pipelined(nb * N_HEADS, logits, update)
        return tuple(out), eq_before

    def run(ties):
        acc_ref[...] = jnp.zeros_like(acc_ref)
        carry = (tuple(jnp.full((1, tq), NEG, F32) for _ in range(N_HEADS)), jnp.zeros((1, tq), F32))
        carry = _sweep(qb, lambda kb, c, nb: attend(kb, c, nb, ties=ties), carry, SWEEP_UNITS // N_HEADS)
        attend(qb, carry, 1, diag=True, ties=ties)

    @pl.when(has_ties)
    def _():
        run(True)

    @pl.when(jnp.logical_not(has_ties))
    def _():
        run(False)

    o_ref[...] = jnp.transpose(jnp.concatenate([_normalized(acc_ref, h) for h in range(N_HEADS)],
                                               axis=0)).astype(o_ref.dtype)


def _attn_c(main3, qaug, wt, k, ki, vt, corr, diagm, slopes, tril, n_sel):
    b, s_len, _ = k.shape
    nq = s_len // TQ
    return pl.pallas_call(
        functools.partial(_attn_c_kernel, n_sel=n_sel),
        grid=(b, nq),
        in_specs=[
            pl.BlockSpec((None, TQ, 256), lambda i, j: (i, j, 3)),
            pl.BlockSpec((None, TQ, 256), lambda i, j: (i, j, 6)),
            pl.BlockSpec(qaug.shape, lambda i, j: (0, 0, 0)),
            pl.BlockSpec((None, IDX_HEADS, TQ), lambda i, j: (i, 0, j)),
            pl.BlockSpec((None, s_len, N_HEADS * LANES), lambda i, j: (i, 0, 0)),
            pl.BlockSpec((None, s_len, IDX_DIM), lambda i, j: (i, 0, 0)),
            pl.BlockSpec((None, s_len, N_HEADS * LANES), lambda i, j: (i, 0, 0)),
            pl.BlockSpec((N_HEADS, TQ, TQ), lambda i, j: (0, 0, 0)),
            pl.BlockSpec((TQ, TQ), lambda i, j: (0, 0)),
            pl.BlockSpec(memory_space=pltpu.SMEM),
            pl.BlockSpec((TQ, TQ), lambda i, j: (0, 0)),
        ],
        out_specs=pl.BlockSpec((None, TQ, 256), lambda i, j: (i, j, 0)),
        out_shape=jax.ShapeDtypeStruct((b, s_len, 256), BF16),
        scratch_shapes=[pltpu.VMEM((s_len, TQ), I32),
                        pltpu.VMEM((s_len, TQ), I16),
                        pltpu.VMEM((s_len, TQ), I16),
                        pltpu.VMEM((N_HEADS, LANES, TQ), F32),
                        pltpu.VMEM((N_HEADS * LANES, TQ), BF16),
                        pltpu.VMEM((IDX_HEADS * IDX_DIM, TQ), BF16)],
        compiler_params=_cparams(("arbitrary", "arbitrary")),
        name="attn_sparse",
    )(main3, main3, qaug, wt, k, ki, vt, corr, diagm, slopes, tril)


def _band_bias_kernel(rb_ref, o_ref):
    h = pl.program_id(0)
    nk, tq = o_ref.shape
    j = lax.broadcasted_iota(I32, (nk, tq), 0)
    t = lax.broadcasted_iota(I32, (nk, tq), 1) + (nk - tq)
    idx = jnp.clip(t - j, -REL_CLIP, REL_CLIP) + REL_CLIP
    cq, ck = t // CHUNK, j // CHUNK
    valid = (ck <= cq) & (ck >= cq - BAND_CHUNKS)
    tbl = lax.fori_loop(0, 2 * REL_CLIP + 1,
                        lambda r, tb: jnp.where(idx == r, rb_ref[h, r], tb), jnp.zeros((nk, tq), F32))
    o_ref[...] = jnp.where(valid, tbl, NEG)


def _band_bias(rel_bias):
    nh = rel_bias.shape[0]
    return pl.pallas_call(
        _band_bias_kernel,
        grid=(nh,),
        in_specs=[pl.BlockSpec(memory_space=pltpu.SMEM)],
        out_specs=pl.BlockSpec((None, 3 * TQ, TQ), lambda h: (h, 0, 0)),
        out_shape=jax.ShapeDtypeStruct((nh, 3 * TQ, TQ), F32),
        compiler_params=_cparams(("arbitrary",)),
        name="band_bias",
    )(rel_bias)


def _attn_d_kernel(q_ref, k_ref, vt_ref, bias_ref, o_ref, qt_ref):
    qb = pl.program_id(1)
    tq = o_ref.shape[0]
    outs = [None] * N_HEADS
    _load_qt(q_ref, qt_ref, HEAD_DIM)

    def logits(h):
        ds = []
        for j in range(3):
            k0 = pl.multiple_of((qb + j) * tq, tq)
            ds.append(_dot(k_ref[pl.ds(k0, tq), h * LANES:(h + 1) * LANES], qt_ref[h * LANES:(h + 1) * LANES, :]))
        return ds

    def attend(h, ds):
        ss = []
        for j in range(3):
            s = bias_ref[h, j * tq:(j + 1) * tq, :] + ds[j]
            ss.append(jnp.where(qb + j >= 2, s, NEG))
        m = jnp.maximum(jnp.maximum(jnp.max(ss[0], axis=0, keepdims=True),
                                    jnp.max(ss[1], axis=0, keepdims=True)),
                        jnp.max(ss[2], axis=0, keepdims=True))
        acc = jnp.zeros((LANES, tq), F32)
        for j in range(3):
            k0 = pl.multiple_of((qb + j) * tq, tq)
            vt = vt_ref[pl.ds(k0, tq), h * LANES:(h + 1) * LANES]
            acc = acc + _dot_t(vt, jnp.exp(ss[j] - m).astype(BF16))
        outs[h] = acc[:HEAD_DIM] / acc[HEAD_DIM:HEAD_DIM + 1]

    _pipelined(N_HEADS, logits, attend, lookahead=1)
    o_ref[...] = jnp.transpose(jnp.concatenate(outs, axis=0)).astype(o_ref.dtype)


def _attn_d(main3, k_pad, vt_pad, bias):
    b, sp, _ = k_pad.shape
    s_len = sp - 2 * TQ
    nq = s_len // TQ
    return pl.pallas_call(
        _attn_d_kernel,
        grid=(b, nq),
        in_specs=[
            pl.BlockSpec((None, TQ, 256), lambda i, j: (i, j, 7)),
            pl.BlockSpec((None, sp, N_HEADS * LANES), lambda i, j: (i, 0, 0)),
            pl.BlockSpec((None, sp, N_HEADS * LANES), lambda i, j: (i, 0, 0)),
            pl.BlockSpec((N_HEADS, 3 * TQ, TQ), lambda i, j: (0, 0, 0)),
        ],
        out_specs=pl.BlockSpec((None, TQ, 256), lambda i, j: (i, j, 0)),
        out_shape=jax.ShapeDtypeStruct((b, s_len, 256), BF16),
        scratch_shapes=[pltpu.VMEM((N_HEADS * LANES, TQ), BF16)],
        compiler_params=_cparams(("arbitrary", "arbitrary")),
        name="attn_band",
    )(main3, k_pad, vt_pad, bias)


def _outproj_kernel(x_ref, oa_ref, ob_ref, oc_ref, od_ref, w_ref, gate_ref, o_ref):
    y = _dot(oa_ref[...], w_ref[0:256, :])
    y = y + _dot(ob_ref[...], w_ref[256:512, :])
    y = y + _dot(oc_ref[...], w_ref[512:768, :])
    y = y + _dot(od_ref[...], w_ref[768:1024, :])
    o_ref[...] = x_ref[...] + gate_ref[...] * y


def _outproj(x2, oa, ob, oc, od, w, mod3, s_len):
    n, d = x2.shape
    tm = TM_PROJ
    tpb = s_len // tm
    ospec = pl.BlockSpec((tm, 256), lambda i: (i, 0))
    return pl.pallas_call(
        _outproj_kernel,
        grid=(n // tm,),
        in_specs=[pl.BlockSpec((tm, d), lambda i: (i, 0)), ospec, ospec, ospec, ospec,
                  pl.BlockSpec((d, d), lambda i: (0, 0)),
                  pl.BlockSpec((None, 1, d), lambda i: (i // tpb, 0, 2))],
        out_specs=pl.BlockSpec((tm, d), lambda i: (i, 0)),
        out_shape=jax.ShapeDtypeStruct((n, d), F32),
        compiler_params=_cparams(("arbitrary",)),
        name="outproj",
    )(x2, oa, ob, oc, od, w, mod3)


def _ffn_kernel(x_ref, g_ref, sh_ref, sc_ref, gate_ref, w1_ref, w3_ref, w2_ref, o_ref,
                hn_ref, acc_ref):
    f = pl.program_id(1)

    @pl.when(f == 0)
    def _():
        hn_ref[...] = _adaln(x_ref[...], g_ref[...], sh_ref[...], sc_ref[...]).astype(BF16)
        acc_ref[...] = jnp.zeros_like(acc_ref)

    hn = hn_ref[...]
    a = _silu(_dot(hn, w1_ref[...])) * _dot(hn, w3_ref[...])
    acc_ref[...] += _dot(a.astype(BF16), w2_ref[...])

    @pl.when(f == pl.num_programs(1) - 1)
    def _():
        o_ref[...] = x_ref[...] + gate_ref[...] * acc_ref[...]


def _ffn(x2, g, mod3, w1, w3, w2, s_len):
    n, d = x2.shape
    dff = w1.shape[1]
    tm = TM_FFN
    tf = dff // 2
    tpb = s_len // tm
    return pl.pallas_call(
        _ffn_kernel,
        grid=(n // tm, dff // tf),
        in_specs=[
            pl.BlockSpec((tm, d), lambda i, f: (i, 0)),
            pl.BlockSpec((1, d), lambda i, f: (0, 0)),
            pl.BlockSpec((None, 1, d), lambda i, f: (i // tpb, 0, 3)),
            pl.BlockSpec((None, 1, d), lambda i, f: (i // tpb, 0, 4)),
            pl.BlockSpec((None, 1, d), lambda i, f: (i // tpb, 0, 5)),
            pl.BlockSpec((d, tf), lambda i, f: (0, f)),
            pl.BlockSpec((d, tf), lambda i, f: (0, f)),
            pl.BlockSpec((tf, d), lambda i, f: (f, 0)),
        ],
        out_specs=pl.BlockSpec((tm, d), lambda i, f: (i, 0)),
        out_shape=jax.ShapeDtypeStruct((n, d), F32),
        scratch_shapes=[pltpu.VMEM((tm, d), BF16), pltpu.VMEM((tm, d), F32)],
        compiler_params=_cparams(("arbitrary", "arbitrary")),
        name="ffn_dense",
    )(x2, g, mod3, mod3, mod3, w1, w3, w2)


def _router_kernel(x_ref, g_ref, sh_ref, sc_ref, rhi_ref, rlo_ref, tril_ref,
                   hn_ref, comb_ref, pos_ref, post_ref, cum_ref):
    sub = tril_ref.shape[0]
    carry = jnp.zeros((1, LANES), F32)
    cum_ref[...] = jnp.zeros_like(cum_ref)
    for r in range(x_ref.shape[0] // sub):
        rows = slice(r * sub, (r + 1) * sub)
        hn = _adaln(x_ref[rows, :], g_ref[...], sh_ref[...], sc_ref[...])
        hi = hn.astype(BF16)
        hn_ref[rows, :] = hi
        lo = (hn - hi.astype(F32)).astype(BF16)
        logits = _dot(hi, rhi_ref[...]) + (_dot(hi, rlo_ref[...]) + _dot(lo, rhi_ref[...]))
        lane = lax.broadcasted_iota(I32, logits.shape, 1)
        logits = jnp.where(lane < N_EXPERTS, logits, NEG)
        m1 = jnp.max(logits, axis=-1, keepdims=True)
        i1 = jnp.min(jnp.where(logits == m1, lane, LANES), axis=-1, keepdims=True)
        rest = jnp.where(lane == i1, NEG, logits)
        m2 = jnp.max(rest, axis=-1, keepdims=True)
        i2 = jnp.min(jnp.where(rest == m2, lane, LANES), axis=-1, keepdims=True)
        e2 = jnp.exp(m2 - m1)
        g1 = 1.0 / (1.0 + e2)
        g2 = e2 / (1.0 + e2)
        comb_ref[rows, :] = jnp.where(lane == i1, g1, 0.0) + jnp.where(lane == i2, g2, 0.0)
        routed = jnp.where((lane == i1) | (lane == i2), 1.0, 0.0)
        incl = _dot(tril_ref[...], routed.astype(BF16)) + carry
        pos = jnp.where(routed > 0.0, incl - 1.0, -1.0)
        pos_ref[rows, :] = pos
        post_ref[:, rows] = jnp.transpose(pos)[:N_EXPERTS, :]
        carry = carry + jnp.sum(routed, axis=0, keepdims=True)
        cum_ref[r + 1:r + 2, :] = carry


def _router(x2, g, mod3, rhi, rlo, tril, s_len):
    n, d = x2.shape
    tm = min(TM_MOE, s_len)
    tpb = s_len // tm
    ns = n // tm
    return pl.pallas_call(
        _router_kernel,
        grid=(ns,),
        in_specs=[
            pl.BlockSpec((tm, d), lambda i: (i, 0)),
            pl.BlockSpec((1, d), lambda i: (0, 0)),
            pl.BlockSpec((None, 1, d), lambda i: (i // tpb, 0, 3)),
            pl.BlockSpec((None, 1, d), lambda i: (i // tpb, 0, 4)),
            pl.BlockSpec((d, LANES), lambda i: (0, 0)),
            pl.BlockSpec((d, LANES), lambda i: (0, 0)),
            pl.BlockSpec(tril.shape, lambda i: (0, 0)),
        ],
        out_specs=[
            pl.BlockSpec((tm, d), lambda i: (i, 0)),
            pl.BlockSpec((tm, LANES), lambda i: (i, 0)),
            pl.BlockSpec((tm, LANES), lambda i: (i, 0)),
            pl.BlockSpec((None, N_EXPERTS, tm), lambda i: (i, 0, 0)),
            pl.BlockSpec((None, 8, LANES), lambda i: (i, 0, 0)),
        ],
        out_shape=[
            jax.ShapeDtypeStruct((n, d), BF16),
            jax.ShapeDtypeStruct((n, LANES), F32),
            jax.ShapeDtypeStruct((n, LANES), F32),
            jax.ShapeDtypeStruct((ns, N_EXPERTS, tm), F32),
            jax.ShapeDtypeStruct((ns, 8, LANES), F32),
        ],
        compiler_params=_cparams(("arbitrary",)),
        name="router",
    )(x2, g, mod3, mod3, rhi, rlo, tril)


def _moe_kernel(cum_ref, hn_ref, comb_ref, pos_ref, post_ref, w1_ref, w3_ref, w2_ref, o_ref,
                xc_ref, y_ref):
    s, e, f = pl.program_id(0), pl.program_id(1), pl.program_id(2)
    nf = pl.num_programs(2)
    tm = hn_ref.shape[0]
    ch, sub = MOE_CHUNK, MOE_SUB
    nsub = tm // sub
    cum = [cum_ref[(s * (nsub + 1) + r) * N_EXPERTS + e] for r in range(nsub + 1)]
    nch = (cum[nsub] + (ch - 1)) // ch

    @pl.when((e == 0) & (f == 0))
    def _():
        o_ref[...] = jnp.zeros_like(o_ref)

    @pl.when(f == 0)
    def _():
        def clear(c, carry):
            r0 = pl.multiple_of(c * ch, ch)
            xc_ref[pl.ds(r0, ch), :] = jnp.zeros((ch, xc_ref.shape[1]), BF16)
            y_ref[pl.ds(r0, ch), :] = jnp.zeros((ch, y_ref.shape[1]), F32)
            return carry
        lax.fori_loop(0, nch + 1, clear, 0)

        slot = lax.broadcasted_iota(I32, (ch, sub), 0).astype(F32)
        for r in range(nsub):
            prow = post_ref[pl.ds(e, 1), r * sub:(r + 1) * sub]

            def gather(c, carry, r=r, prow=prow):
                r0 = pl.multiple_of(c * ch, ch)
                sel = jnp.where(prow - (c * ch).astype(F32) == slot, 1.0, 0.0).astype(BF16)
                rows = _dot(sel, hn_ref[r * sub:(r + 1) * sub, :])
                xc_ref[pl.ds(r0, ch), :] = (xc_ref[pl.ds(r0, ch), :].astype(F32) + rows).astype(BF16)
                return carry
            lax.fori_loop(cum[r] // ch, (cum[r + 1] + (ch - 1)) // ch, gather, 0)

    def expert(c, carry):
        r0 = pl.multiple_of(c * ch, ch)
        xc = xc_ref[pl.ds(r0, ch), :]
        a = _silu(_dot(xc, w1_ref[...])) * _dot(xc, w3_ref[...])
        y_ref[pl.ds(r0, ch), :] += _dot(a.astype(BF16), w2_ref[...])
        return carry
    lax.fori_loop(0, nch, expert, 0)

    @pl.when(f == nf - 1)
    def _():
        lane = lax.broadcasted_iota(I32, (tm, LANES), 1)
        pcol = jnp.sum(jnp.where(lane == e, pos_ref[...], 0.0), axis=-1, keepdims=True)
        gcol = jnp.sum(jnp.where(lane == e, comb_ref[...], 0.0), axis=-1, keepdims=True)
        slot = lax.broadcasted_iota(I32, (sub, 2 * ch), 1).astype(F32)
        for r in range(nsub):
            rows = slice(r * sub, (r + 1) * sub)

            def scatter(c2, carry, rows=rows):
                r0 = pl.multiple_of(c2 * (2 * ch), 2 * ch)
                z = y_ref[pl.ds(r0, 2 * ch), :].astype(BF16)
                sel = jnp.where(pcol[rows] - (c2 * (2 * ch)).astype(F32) == slot, 1.0, 0.0).astype(BF16)
                o_ref[rows, :] += gcol[rows] * _dot(sel, z)
                return carry
            lax.fori_loop(cum[r] // (2 * ch), (cum[r + 1] + (2 * ch - 1)) // (2 * ch), scatter, 0)


def _moe(hn, comb, pos, post, cum, w1, w3, w2):
    n, d = hn.shape
    ne, _, dff = w1.shape
    ns, _, tm = post.shape
    tf = dff // 4
    cap = tm + 2 * MOE_CHUNK
    grid_spec = pltpu.PrefetchScalarGridSpec(
        num_scalar_prefetch=1,
        grid=(ns, ne, dff // tf),
        in_specs=[
            pl.BlockSpec((tm, d), lambda i, e, f, c: (i, 0)),
            pl.BlockSpec((tm, LANES), lambda i, e, f, c: (i, 0)),
            pl.BlockSpec((tm, LANES), lambda i, e, f, c: (i, 0)),
            pl.BlockSpec((None, N_EXPERTS, tm), lambda i, e, f, c: (i, 0, 0)),
            pl.BlockSpec((None, d, tf), lambda i, e, f, c: (e, 0, f)),
            pl.BlockSpec((None, d, tf), lambda i, e, f, c: (e, 0, f)),
            pl.BlockSpec((None, tf, d), lambda i, e, f, c: (e, f, 0)),
        ],
        out_specs=pl.BlockSpec((tm, d), lambda i, e, f, c: (i, 0)),
        scratch_shapes=[pltpu.VMEM((cap, d), BF16), pltpu.VMEM((cap, d), F32)],
    )
    return pl.pallas_call(
        _moe_kernel,
        grid_spec=grid_spec,
        out_shape=jax.ShapeDtypeStruct((n, d), F32),
        compiler_params=_cparams(("arbitrary", "arbitrary", "arbitrary")),
        name="ffn_experts",
    )(cum, hn, comb, pos, post, w1, w3, w2)


def _residual_kernel(x_ref, f_ref, gate_ref, gn_ref, o_ref, *, final):
    y = x_ref[...] + gate_ref[...] * f_ref[...]
    o_ref[...] = _rms(y, gn_ref[...]) if final else y


def _residual(x2, f2, mod3, gn, s_len, final):
    n, d = x2.shape
    tm = TM_FFN
    tpb = s_len // tm
    row = pl.BlockSpec((tm, d), lambda i: (i, 0))
    return pl.pallas_call(
        functools.partial(_residual_kernel, final=final),
        grid=(n // tm,),
        in_specs=[row, row, pl.BlockSpec((None, 1, d), lambda i: (i // tpb, 0, 5)),
                  pl.BlockSpec((1, d), lambda i: (0, 0))],
        out_specs=row,
        out_shape=jax.ShapeDtypeStruct((n, d), F32),
        compiler_params=_cparams(("arbitrary",)),
        name="residual",
    )(x2, f2, mod3, gn)


def _final_kernel(x_ref, g_ref, o_ref):
    o_ref[...] = _rms(x_ref[...], g_ref[...])


def _final_norm(x2, g):
    n, d = x2.shape
    tm = TM_FFN
    return pl.pallas_call(
        _final_kernel,
        grid=(n // tm,),
        in_specs=[pl.BlockSpec((tm, d), lambda i: (i, 0)), pl.BlockSpec((1, d), lambda i: (0, 0))],
        out_specs=pl.BlockSpec((tm, d), lambda i: (i, 0)),
        out_shape=jax.ShapeDtypeStruct((n, d), F32),
        compiler_params=_cparams(("arbitrary",)),
        name="final_norm",
    )(x2, g)


def _alibi_slopes():
    return 2.0 ** (-8.0 * jnp.arange(1, N_HEADS + 1, dtype=F32) / N_HEADS)


def _block_tables():
    j = jnp.arange(TQ)[:, None]
    i = jnp.arange(TQ)[None, :]
    diagm = jnp.where((j // CHUNK) <= (i // CHUNK), 0.0, NEG).astype(F32)
    corr = -2.0 * _alibi_slopes()[:, None, None] * jnp.maximum(j - i, 0).astype(F32)[None]
    return corr, diagm


def _rope_tables(s_len):
    inv = ROPE_THETA ** (-jnp.arange(0, QK_ROPE, 2, dtype=F32) / QK_ROPE)
    ang = jnp.arange(s_len, dtype=F32)[:, None] * inv[None, :]
    cos2 = jnp.concatenate([jnp.cos(ang)] * 2, axis=-1)
    sin2 = jnp.concatenate([jnp.sin(ang)] * 2, axis=-1)
    z = jnp.zeros((s_len, QK_NOPE), F32)
    z2 = jnp.zeros((s_len, LANES - QK_NOPE - QK_ROPE), F32)
    return (jnp.concatenate([z, cos2, z2], axis=-1), jnp.concatenate([z, sin2, z2], axis=-1))


def _rot_cols(w):
    half = QK_ROPE // 2
    return jnp.concatenate([-w[..., half:], w[..., :half]], axis=-1)


def _slabs(a, nh, extra=None):
    b, s_len, _ = a.shape
    parts = [a.reshape(b, s_len, nh, -1)]
    if extra is not None:
        parts.append(jnp.broadcast_to(extra, (b, s_len, nh, extra.shape[-1])).astype(a.dtype))
    used = sum(p.shape[-1] for p in parts)
    parts.append(jnp.zeros((b, s_len, nh, LANES - used), a.dtype))
    return jnp.concatenate(parts, axis=-1)


def _rows(slab):
    b, s_len = slab.shape[:2]
    return slab.reshape(b, s_len, -1)


def _alibi_extras(s_len, nslab):
    loc = (jnp.arange(s_len) % TQ).astype(F32)[:, None, None]
    k_extra = jnp.concatenate([jnp.ones((s_len, nslab, 1), F32),
                               jnp.broadcast_to(loc, (s_len, nslab, 1))], axis=-1)
    sl = _alibi_slopes()[:, None, None]
    t_loc = jnp.arange(TQ, dtype=F32)[None, None, :]
    q_rows = jnp.concatenate([-sl * t_loc, jnp.broadcast_to(sl, (N_HEADS, 1, TQ)),
                              jnp.zeros((N_HEADS, 6, TQ), F32)], axis=1)
    return q_rows, k_extra[None]


def _layer_weights(l, w_in, mla_w_uq, mla_w_ukv):
    w = w_in[l]
    names = ('a_q', 'a_k', 'a_v', 'b_qd', 'b_kvd', 'b_kr', 'c_q', 'c_k', 'c_v', 'c_qi', 'c_ki',
             'c_wi', 'd_q', 'd_k', 'd_v')
    widths = (256, 256, 256, Q_LORA, KV_LORA, QK_ROPE, 256, 256, 256, IDX_HEADS * IDX_DIM, IDX_DIM,
              IDX_HEADS, 256, 256, 256)
    cols, o = {}, 0
    for nme, wd in zip(names, widths):
        cols[nme] = w[:, o:o + wd]
        o += wd
    pad = jnp.zeros((w.shape[0], N_SIDE - (Q_LORA + KV_LORA + 2 * QK_ROPE + IDX_DIM + IDX_HEADS)), F32)
    wcat = jnp.concatenate(
        [cols['a_q'], cols['a_k'], cols['a_v'], cols['c_q'], cols['c_k'], cols['c_v'], cols['c_qi'],
         cols['d_q'], cols['d_k'], cols['d_v'],
         cols['b_qd'], cols['b_kvd'], cols['b_kr'], _rot_cols(cols['b_kr']), cols['c_ki'],
         cols['c_wi'], pad], axis=-1).astype(BF16)
    ones = lambda k: jnp.ones((k,), F32)
    cs = jnp.concatenate([
        ones(256) * DIFF_D ** -0.5, ones(512),
        ones(256) * HEAD_DIM ** -0.5, ones(768),
        ones(256) * HEAD_DIM ** -0.5, ones(512), ones(N_SIDE)])[None, :]

    uq = mla_w_uq[l].reshape(Q_LORA, N_HEADS, QK_NOPE + QK_ROPE)
    zq = jnp.zeros((Q_LORA, N_HEADS, LANES - QK_NOPE - QK_ROPE), F32)
    wqa = jnp.concatenate([uq, zq], axis=-1).reshape(Q_LORA, N_HEADS * LANES).astype(BF16)
    wqb = jnp.concatenate([jnp.zeros((Q_LORA, N_HEADS, QK_NOPE), F32), _rot_cols(uq[..., QK_NOPE:]), zq],
                          axis=-1).reshape(Q_LORA, N_HEADS * LANES).astype(BF16)
    ukv = mla_w_ukv[l].reshape(KV_LORA, N_HEADS, QK_NOPE + HEAD_DIM)
    zk = jnp.zeros((KV_LORA, N_HEADS, LANES - QK_NOPE), F32)
    wk = jnp.concatenate([ukv[..., :QK_NOPE], zk], axis=-1).reshape(KV_LORA, N_HEADS * LANES).astype(BF16)
    wv = jnp.concatenate([ukv[..., QK_NOPE:], zk], axis=-1).reshape(KV_LORA, N_HEADS * LANES).astype(BF16)
    return wcat, cs, wqa, wqb, wk, wv


def _placement():
    r = jnp.arange(LANES)[:, None]
    c = jnp.arange(N_HEADS * LANES)[None, :] % LANES
    e1 = ((c >= QK_NOPE) & (c < QK_NOPE + QK_ROPE) & (r == c - QK_NOPE)).astype(BF16)
    e2 = ((c >= QK_NOPE) & (c < QK_NOPE + QK_ROPE) & (r == c - QK_NOPE + QK_ROPE)).astype(BF16)
    return e1, e2


def kernel(x, c, ada_w, ada_b, mix_norm_g, ffn_norm_g, w_in, w_out, diff_lambda, diff_norm_g, mla_q_norm_g, mla_kv_norm_g, mla_w_uq, mla_w_ukv, band_rel_bias, ffn_w1, ffn_w3, ffn_w2, moe_router, moe_w1, moe_w3, moe_w2, final_norm_g):
    b, s_len, d = x.shape
    depth = ada_w.shape[0]
    n = b * s_len
    n_sel = min(TOPK_MAX, s_len // 4)
    assert d == D_MODEL and s_len % TM_FFN == 0 and s_len % TQ == 0

    mod = _modulation(c, ada_w, ada_b)
    corr, diagm = _block_tables()
    slopes = _alibi_slopes()
    cos_t, sin_t = _rope_tables(s_len)
    e1, e2 = _placement()
    tril = (jnp.arange(TQ)[None, :] <= jnp.arange(TQ)[:, None]).astype(BF16)
    mla_scale = (QK_NOPE + QK_ROPE) ** -0.5
    qaug, kx8 = _alibi_extras(s_len, 2 * N_HEADS)
    _, kx4 = _alibi_extras(s_len, N_HEADS)
    one4 = jnp.ones((1, s_len, N_HEADS, 1), F32)
    tr = lambda a: jnp.swapaxes(a, 1, 2)
    front = 2 * TQ

    x2 = x.reshape(n, d)
    for l in range(depth):
        mod3 = mod[l].reshape(b, 1, 6 * d)
        wcat, cs, wqa, wqb, wk, wv = _layer_weights(l, w_in, mla_w_uq, mla_w_ukv)
        main, side = _inproj(x2, mix_norm_g[l][None], mod3, wcat, cs, s_len)
        main3 = main.reshape(b, s_len, N_MAIN)
        side3 = side.reshape(b, s_len, N_SIDE)
        blk = lambda k: main3[:, :, k * 256:(k + 1) * 256]

        lam_init = 0.8 - 0.6 * math.exp(-0.3 * l)
        o_a = _attn_a(main3, qaug, _rows(_slabs(blk(1), 2 * N_HEADS, kx8)),
                      _rows(_slabs(blk(2), N_HEADS, one4)), corr + diagm[None], slopes, diff_lambda[l],
                      jnp.broadcast_to(diff_norm_g[l][:, None], (2 * DIFF_D, TQ)), lam_init)

        qb_, kb_, vb_ = _mla_prep(side, mla_q_norm_g[l][None], mla_kv_norm_g[l][None], wqa, wqb, wk, wv,
                                  e1, e2, cos_t, sin_t, s_len, mla_scale)
        o_b = _attn_b(qb_.reshape(b, s_len, -1), kb_.reshape(b, s_len, -1),
                      vb_.reshape(b, s_len, -1), diagm)

        o0 = Q_LORA + KV_LORA + 2 * QK_ROPE
        ki = side3[:, :, o0:o0 + IDX_DIM].astype(BF16)
        wt = tr(side3[:, :, o0 + IDX_DIM:o0 + IDX_DIM + IDX_HEADS])
        o_c = _attn_c(main3, qaug, wt, _rows(_slabs(blk(4), N_HEADS, kx4)),
                      ki, _rows(_slabs(blk(5), N_HEADS, one4)), corr, diagm, slopes, tril, n_sel)

        k_pad = jnp.pad(_rows(_slabs(blk(8), N_HEADS)), ((0, 0), (front, 0), (0, 0)))
        vt_pad = jnp.pad(_rows(_slabs(blk(9), N_HEADS, one4)), ((0, 0), (front, 0), (0, 0)))
        o_d = _attn_d(main3, k_pad, vt_pad, _band_bias(band_rel_bias[l]))

        flat = lambda a: a.reshape(n, 256)
        x2 = _outproj(x2, flat(o_a), flat(o_b), flat(o_c), flat(o_d), w_out[l].astype(BF16), mod3, s_len)

        gf = ffn_norm_g[l][None]
        if l % 2 == 0:
            i = l // 2
            x2 = _ffn(x2, gf, mod3, ffn_w1[i].astype(BF16), ffn_w3[i].astype(BF16),
                      ffn_w2[i].astype(BF16), s_len)
        else:
            i = l // 2
            r = jnp.pad(moe_router[i], ((0, 0), (0, LANES - N_EXPERTS)))
            rhi = r.astype(BF16)
            rlo = (r - rhi.astype(F32)).astype(BF16)
            tril_sub = (jnp.arange(MOE_SUB)[None, :] <= jnp.arange(MOE_SUB)[:, None]).astype(BF16)
            hn, comb, pos, post, cum = _router(x2, gf, mod3, rhi, rlo, tril_sub, s_len)
            nsub = post.shape[-1] // MOE_SUB
            cum_i = cum[:, :nsub + 1, :N_EXPERTS].astype(I32).reshape(-1)
            f2 = _moe(hn, comb, pos, post, cum_i, moe_w1[i].astype(BF16), moe_w3[i].astype(BF16),
                      moe_w2[i].astype(BF16))
            last = l == depth - 1
            x2 = _residual(x2, f2, mod3, final_norm_g[None], s_len, last)
            if last:
                return x2.reshape(b, s_len, d)
    return _final_norm(x2, final_norm_g[None]).reshape(b, s_len, d)
```

```python
import functools
import math

import numpy as np
import jax
import jax.numpy as jnp
from jax import lax
from jax.experimental import pallas as pl
from jax.experimental.pallas import tpu as pltpu

F32 = jnp.float32
BF16 = jnp.bfloat16
I32 = jnp.int32
I16 = jnp.int16

D_MODEL = 1024
CHUNK = 64
N_HEADS = 4
HEAD_DIM = 64
DIFF_D = 32
Q_LORA = 256
KV_LORA = 128
QK_NOPE = 64
QK_ROPE = 32
ROPE_THETA = 10000.0
IDX_HEADS = 8
IDX_DIM = 32
TOPK_MAX = 256
BAND_CHUNKS = 8
REL_CLIP = 128
N_EXPERTS = 8
EPS = 1e-6

NEG = -1e30
INT_MIN = -(2 ** 31)
I16_MIN = -(2 ** 15)

LANES = 128
PACK16 = 16
VMEM_LIMIT = 56 * 1024 * 1024
TQ = 256
SWEEP_UNITS = 16
TM_PROJ = 512
TM_FFN = 512
TM_MOE = 2048
MOE_CHUNK = 128
MOE_SUB = 512
N_MAIN = 2560
N_SIDE = 512


def _cparams(sem):
    return pltpu.CompilerParams(dimension_semantics=sem, vmem_limit_bytes=VMEM_LIMIT)


def _dot(a, b):
    return jnp.dot(a, b, preferred_element_type=F32)


def _silu(x):
    return x / (1.0 + jnp.exp(-x))


def _rms(x, g):
    return x * lax.rsqrt(jnp.mean(x * x, axis=-1, keepdims=True) + EPS) * g


def _adaln(x, g, shift, scale):
    return _rms(x, g) * (1.0 + scale) + shift


def _mod_kernel(c_ref, w_ref, b_ref, o_ref):
    sc = _silu(c_ref[...]).astype(BF16)
    o_ref[...] = _dot(sc, w_ref[...].astype(BF16)) + b_ref[...]


def _modulation(c, ada_w, ada_b):
    depth, d, n6 = ada_w.shape
    b = c.shape[0]
    tn = 1536
    return pl.pallas_call(
        _mod_kernel,
        grid=(depth, n6 // tn),
        in_specs=[
            pl.BlockSpec((b, d), lambda l, j: (0, 0)),
            pl.BlockSpec((None, d, tn), lambda l, j: (l, 0, j)),
            pl.BlockSpec((None, 1, tn), lambda l, j: (l, 0, j)),
        ],
        out_specs=pl.BlockSpec((None, b, tn), lambda l, j: (l, 0, j)),
        out_shape=jax.ShapeDtypeStruct((depth, b, n6), F32),
        compiler_params=_cparams(("arbitrary", "arbitrary")),
        name="modulation",
    )(c, ada_w, ada_b.reshape(depth, 1, n6))


def _inproj_kernel(x_ref, g_ref, sh_ref, sc_ref, w_ref, cs_ref, main_ref, side_ref):
    hn = _adaln(x_ref[...], g_ref[...], sh_ref[...], sc_ref[...]).astype(BF16)
    step = 512
    for c0 in range(0, N_MAIN + N_SIDE, step):
        acc = _dot(hn, w_ref[:, c0:c0 + step]) * cs_ref[:, c0:c0 + step]
        if c0 < N_MAIN:
            main_ref[:, c0:c0 + step] = acc.astype(BF16)
        else:
            side_ref[:, c0 - N_MAIN:c0 - N_MAIN + step] = acc


def _inproj(x2, g, mod3, w, cs, s_len):
    n, d = x2.shape
    tm = TM_PROJ
    tpb = s_len // tm
    nc = N_MAIN + N_SIDE
    return pl.pallas_call(
        _inproj_kernel,
        grid=(n // tm,),
        in_specs=[
            pl.BlockSpec((tm, d), lambda i: (i, 0)),
            pl.BlockSpec((1, d), lambda i: (0, 0)),
            pl.BlockSpec((None, 1, d), lambda i: (i // tpb, 0, 0)),
            pl.BlockSpec((None, 1, d), lambda i: (i // tpb, 0, 1)),
            pl.BlockSpec((d, nc), lambda i: (0, 0)),
            pl.BlockSpec((1, nc), lambda i: (0, 0)),
        ],
        out_specs=[
            pl.BlockSpec((tm, N_MAIN), lambda i: (i, 0)),
            pl.BlockSpec((tm, N_SIDE), lambda i: (i, 0)),
        ],
        out_shape=[
            jax.ShapeDtypeStruct((n, N_MAIN), BF16),
            jax.ShapeDtypeStruct((n, N_SIDE), F32),
        ],
        compiler_params=_cparams(("arbitrary",)),
        name="inproj",
    )(x2, g, mod3, mod3, w, cs)


def _dot_t(a, b):
    return lax.dot_general(a, b, (((0,), (0,)), ((), ())), preferred_element_type=F32)


def _softmax_update(s, c, m, acc_ref, slot, v):
    m_new = jnp.maximum(m, jnp.max(s, axis=0, keepdims=True) + c)
    alpha = jnp.exp(m - m_new)
    p = jnp.exp((s - (m_new - c)).astype(BF16))
    acc_ref[slot] = acc_ref[slot] * alpha + _dot_t(v, p)
    return m_new


def _normalized(acc_ref, slot):
    a = acc_ref[slot]
    return a[:HEAD_DIM] / a[HEAD_DIM:HEAD_DIM + 1]


def _pipelined(n, produce, consume, lookahead=3):
    vals = [produce(i) for i in range(min(lookahead, n))]
    for i in range(n):
        if i + lookahead < n:
            vals.append(produce(i + lookahead))
        consume(i, vals[i])
        vals[i] = None


def _load_qt(q_ref, qt_ref, width, qaug_ref=None, slabs_per_aug=1):
    tq = q_ref.shape[0]
    qt = jnp.transpose(q_ref[...].astype(F32))
    for i in range(qt.shape[0] // width):
        parts = [qt[i * width:(i + 1) * width]]
        if qaug_ref is not None:
            parts.append(qaug_ref[i // slabs_per_aug])
        used = sum(p.shape[0] for p in parts)
        if used < LANES:
            parts.append(jnp.zeros((LANES - used, tq), F32))
        qt_ref[i * LANES:(i + 1) * LANES, :] = jnp.concatenate(parts, axis=0).astype(BF16)


def _sweep(qb, body, carry, group):
    ng = qb // group
    carry = lax.fori_loop(0, ng, lambda g, c: body(g * group, c, group), carry)
    return lax.fori_loop(ng * group, qb, lambda kb, c: body(kb, c, 1), carry)


def _block_const(slope_ref, h, qb, kb, tq):
    return -slope_ref[h] * ((qb - kb) * tq).astype(F32)


def _attn_a_kernel(q_ref, qaug_ref, k_ref, vt_ref, corr_ref, slope_ref, lamv_ref, g_ref, o_ref, acc_ref,
                   qt_ref, *, lam_init):
    qb = pl.program_id(1)
    tq = o_ref.shape[0]
    nhm = 2 * N_HEADS
    acc_ref[...] = jnp.zeros_like(acc_ref)
    _load_qt(q_ref, qt_ref, DIFF_D, qaug_ref, slabs_per_aug=2)

    def block(kb, ms, nb, diag=False):
        out = list(ms)
        start = lambda u: pl.multiple_of((kb + u // nhm) * tq, tq)

        def logits(u):
            i = u % nhm
            return _dot(k_ref[pl.ds(start(u), tq), i * LANES:(i + 1) * LANES], qt_ref[i * LANES:(i + 1) * LANES, :])

        def update(u, s):
            i = u % nhm
            h = i // 2
            vt = vt_ref[pl.ds(start(u), tq), h * LANES:(h + 1) * LANES]
            if diag:
                s, c = corr_ref[h] + s, 0.0
            else:
                c = _block_const(slope_ref, h, qb, kb + u // nhm, tq)
            out[i] = _softmax_update(s, c, out[i], acc_ref, i, vt)

        _pipelined(nb * nhm, logits, update)
        return tuple(out)

    ms = tuple(jnp.full((1, tq), NEG, F32) for _ in range(nhm))
    ms = block(qb, ms, 1, diag=True)
    _sweep(qb, block, ms, SWEEP_UNITS // nhm)

    lv = lamv_ref[...]
    lam = (jnp.exp(jnp.sum(lv[0:1] * lv[1:2], axis=-1, keepdims=True))
           - jnp.exp(jnp.sum(lv[2:3] * lv[3:4], axis=-1, keepdims=True)) + lam_init)
    outs = []
    for h in range(N_HEADS):
        o = _normalized(acc_ref, 2 * h) - lam * _normalized(acc_ref, 2 * h + 1)
        o = o * lax.rsqrt(jnp.mean(o * o, axis=0, keepdims=True) + EPS) * g_ref[...]
        outs.append(o * (1.0 - lam_init))
    o_ref[...] = jnp.transpose(jnp.concatenate(outs, axis=0)).astype(o_ref.dtype)


def _attn_a(main3, qaug, k, vt, corr, slopes, lamv, gcol, lam_init):
    b, s_len, _ = k.shape
    nq = s_len // TQ
    return pl.pallas_call(
        functools.partial(_attn_a_kernel, lam_init=lam_init),
        grid=(b, nq),
        in_specs=[
            pl.BlockSpec((None, TQ, 256), lambda i, j: (i, j, 0)),
            pl.BlockSpec(qaug.shape, lambda i, j: (0, 0, 0)),
            pl.BlockSpec((None, s_len, 2 * N_HEADS * LANES), lambda i, j: (i, 0, 0)),
            pl.BlockSpec((None, s_len, N_HEADS * LANES), lambda i, j: (i, 0, 0)),
            pl.BlockSpec((N_HEADS, TQ, TQ), lambda i, j: (0, 0, 0)),
            pl.BlockSpec(memory_space=pltpu.SMEM),
            pl.BlockSpec((4, DIFF_D), lambda i, j: (0, 0)),
            pl.BlockSpec((2 * DIFF_D, TQ), lambda i, j: (0, 0)),
        ],
        out_specs=pl.BlockSpec((None, TQ, 256), lambda i, j: (i, j, 0)),
        out_shape=jax.ShapeDtypeStruct((b, s_len, 256), BF16),
        scratch_shapes=[pltpu.VMEM((2 * N_HEADS, LANES, TQ), F32),
                        pltpu.VMEM((2 * N_HEADS * LANES, TQ), BF16)],
        compiler_params=_cparams(("arbitrary", "arbitrary")),
        name="attn_diff",
    )(main3, qaug, k, vt, corr, slopes, lamv, gcol)


def _mla_prep_kernel(side_ref, gq_ref, gkv_ref, wqa_ref, wqb_ref, wk_ref, wv_ref, e1_ref, e2_ref,
                     cos_ref, sin_ref, q_ref, k_ref, v_ref, *, qscale):
    side = side_ref[...]
    hq = _rms(side[:, :Q_LORA], gq_ref[...]).astype(BF16)
    hkv = _rms(side[:, Q_LORA:Q_LORA + KV_LORA], gkv_ref[...]).astype(BF16)
    misc = side[:, Q_LORA + KV_LORA:]
    cos4 = jnp.concatenate([cos_ref[...]] * N_HEADS, axis=-1)
    sin4 = jnp.concatenate([sin_ref[...]] * N_HEADS, axis=-1)
    lane = lax.broadcasted_iota(I32, (1, N_HEADS * LANES), 1) % LANES
    nope = (lane < QK_NOPE).astype(F32)
    ones_col = (lane == HEAD_DIM).astype(F32)
    q = (_dot(hq, wqa_ref[...]) * (nope + cos4) + _dot(hq, wqb_ref[...]) * sin4) * qscale
    q_ref[...] = q.astype(BF16)
    hi = misc.astype(BF16)
    lo = (misc - hi.astype(F32)).astype(BF16)
    kr = _dot(hi, e1_ref[...]) + _dot(lo, e1_ref[...])
    krp = _dot(hi, e2_ref[...]) + _dot(lo, e2_ref[...])
    k = _dot(hkv, wk_ref[...]) + kr * cos4 + krp * sin4
    k_ref[...] = k.astype(BF16)
    v_ref[...] = (_dot(hkv, wv_ref[...]) + ones_col).astype(BF16)


def _mla_prep(side, gq, gkv, wqa, wqb, wk, wv, e1, e2, cos_t, sin_t, s_len, qscale):
    n = side.shape[0]
    tm = TM_PROJ
    tpb = s_len // tm
    full = lambda a: pl.BlockSpec(a.shape, lambda i: (0,) * a.ndim)
    out = jax.ShapeDtypeStruct((n, N_HEADS * LANES), BF16)
    return pl.pallas_call(
        functools.partial(_mla_prep_kernel, qscale=qscale),
        grid=(n // tm,),
        in_specs=[pl.BlockSpec((tm, N_SIDE), lambda i: (i, 0)),
                  full(gq), full(gkv), full(wqa), full(wqb), full(wk), full(wv), full(e1), full(e2),
                  pl.BlockSpec((tm, LANES), lambda i: (i % tpb, 0)),
                  pl.BlockSpec((tm, LANES), lambda i: (i % tpb, 0))],
        out_specs=[pl.BlockSpec((tm, N_HEADS * LANES), lambda i: (i, 0))] * 3,
        out_shape=[out, out, out],
        compiler_params=_cparams(("arbitrary",)),
        name="mla_prep",
    )(side, gq, gkv, wqa, wqb, wk, wv, e1, e2, cos_t, sin_t)


def _attn_b_kernel(q_ref, k_ref, vt_ref, diagm_ref, o_ref, acc_ref, qt_ref):
    qb = pl.program_id(1)
    tq = o_ref.shape[0]
    acc_ref[...] = jnp.zeros_like(acc_ref)
    _load_qt(q_ref, qt_ref, LANES)

    def block(kb, ms, nb, diag=False):
        out = list(ms)
        start = lambda u: pl.multiple_of((kb + u // N_HEADS) * tq, tq)

        def logits(u):
            h = u % N_HEADS
            return _dot(k_ref[pl.ds(start(u), tq), h * LANES:(h + 1) * LANES], qt_ref[h * LANES:(h + 1) * LANES, :])

        def update(u, s):
            h = u % N_HEADS
            vt = vt_ref[pl.ds(start(u), tq), h * LANES:(h + 1) * LANES]
            if diag:
                s = diagm_ref[...] + s
            out[h] = _softmax_update(s, 0.0, out[h], acc_ref, h, vt)

        _pipelined(nb * N_HEADS, logits, update)
        return tuple(out)

    ms = tuple(jnp.full((1, tq), NEG, F32) for _ in range(N_HEADS))
    ms = block(qb, ms, 1, diag=True)
    _sweep(qb, block, ms, SWEEP_UNITS // N_HEADS)
    o_ref[...] = jnp.transpose(jnp.concatenate([_normalized(acc_ref, h) for h in range(N_HEADS)],
                                               axis=0)).astype(o_ref.dtype)


def _attn_b(q3, k, vt, diagm):
    b, s_len, _ = k.shape
    nq = s_len // TQ
    return pl.pallas_call(
        _attn_b_kernel,
        grid=(b, nq),
        in_specs=[
            pl.BlockSpec((None, TQ, N_HEADS * LANES), lambda i, j: (i, j, 0)),
            pl.BlockSpec((None, s_len, N_HEADS * LANES), lambda i, j: (i, 0, 0)),
            pl.BlockSpec((None, s_len, N_HEADS * LANES), lambda i, j: (i, 0, 0)),
            pl.BlockSpec((TQ, TQ), lambda i, j: (0, 0)),
        ],
        out_specs=pl.BlockSpec((None, TQ, 256), lambda i, j: (i, j, 0)),
        out_shape=jax.ShapeDtypeStruct((b, s_len, 256), BF16),
        scratch_shapes=[pltpu.VMEM((N_HEADS, LANES, TQ), F32),
                        pltpu.VMEM((N_HEADS * LANES, TQ), BF16)],
        compiler_params=_cparams(("arbitrary", "arbitrary")),
        name="attn_latent",
    )(q3, k, vt, diagm)


def _attn_c_kernel(q_ref, qi_ref, qaug_ref, wt_ref, k_ref, ki_ref, vt_ref, corr_ref, diagm_ref, slope_ref,
                   tril_ref, o_ref, keys_ref, khi_ref, klo_ref, acc_ref, qt_ref, qit_ref, *, n_sel):
    qb = pl.program_id(1)
    tq = o_ref.shape[0]
    nkb = qb + 1
    _load_qt(q_ref, qt_ref, HEAD_DIM, qaug_ref)
    qit_ref[...] = jnp.transpose(qi_ref[...].astype(F32)).astype(BF16)

    wt = wt_ref[...] * (IDX_HEADS ** -0.5 * IDX_DIM ** -0.5)

    def score_blocks(kb, c, nb, diag=False):
        starts = [pl.multiple_of((kb + j) * tq, tq) for j in range(nb)]
        kis = [ki_ref[pl.ds(st, tq), :] for st in starts]
        totals = [jnp.zeros((tq, tq), F32) for _ in range(nb)]

        def indexer(u):
            j, h = divmod(u, IDX_HEADS)
            return _dot(kis[j], qit_ref[h * IDX_DIM:(h + 1) * IDX_DIM, :])

        def weighted(u, d):
            j, h = divmod(u, IDX_HEADS)
            totals[j] = totals[j] + jnp.maximum(d, 0.0) * wt[h:h + 1]

        _pipelined(nb * IDX_HEADS, indexer, weighted)
        for j in range(nb):
            sc = jnp.where(totals[j] == 0.0, 0.0, totals[j])
            bits = lax.bitcast_convert_type(sc, I32)
            key = bits ^ ((bits >> 31) & 0x7FFFFFFF)
            if diag:
                key = jnp.where(diagm_ref[...] < 0.0, INT_MIN, key)
            keys_ref[pl.ds(starts[j], tq), :] = key
            khi_ref[pl.ds(starts[j], tq), :] = (key >> 16).astype(I16)
            klo_ref[pl.ds(starts[j], tq), :] = ((key & 0xFFFF) - 32768).astype(I16)
        return c

    score_blocks(qb, 0, 1, diag=True)
    _sweep(qb, score_blocks, 0, 2)

    def count_ge(cand):
        def body(kb, cnt8):
            k0 = pl.multiple_of(kb * tq, tq)
            hit = jnp.where(keys_ref[pl.ds(k0, tq), :] >= cand, 1, 0)
            return cnt8 + jnp.sum(hit.reshape(tq // 8, 8, tq), axis=0)
        cnt8 = lax.fori_loop(0, nkb, body, jnp.zeros((8, tq), I32))
        return jnp.sum(cnt8, axis=0, keepdims=True)

    def count16(ref, cand):
        cand16 = cand.astype(I16)

        def body(kb, cnt):
            k0 = pl.multiple_of(kb * tq, tq)
            hit = jnp.where(ref[pl.ds(k0, tq), :] >= cand16, jnp.int16(1), jnp.int16(0))
            for j in range(tq // PACK16):
                cnt = cnt + hit[j * PACK16:(j + 1) * PACK16]
            return cnt
        cnt = lax.fori_loop(0, nkb, body, jnp.zeros((PACK16, tq), I16))
        return jnp.sum(cnt.astype(I32), axis=0, keepdims=True)

    def select16(ref, k):
        zero = jnp.zeros((1, tq), I32)
        t = jnp.where(count16(ref, zero) >= k, zero, I16_MIN)

        def bit_body(i, t):
            cand = t + (jnp.int32(1) << (14 - i))
            return jnp.where(count16(ref, cand) >= k, cand, t)
        return lax.fori_loop(0, 15, bit_body, t)

    thr_hi = select16(khi_ref, n_sel)
    above = jnp.where(thr_hi >= 32767, 0, count16(khi_ref, thr_hi + 1))
    thr_hi16 = thr_hi.astype(I16)

    def _bucket(kb, c):
        k0 = pl.multiple_of(kb * tq, tq)
        klo_ref[pl.ds(k0, tq), :] = jnp.where(khi_ref[pl.ds(k0, tq), :] == thr_hi16,
                                              klo_ref[pl.ds(k0, tq), :], jnp.int16(I16_MIN))
        return c
    lax.fori_loop(0, nkb, _bucket, 0)
    thr_lo = select16(klo_ref, n_sel - above)
    thr = thr_hi * 65536 + (thr_lo + 32768)
    thr = jnp.maximum(thr, INT_MIN + 1)
    c_gt = count_ge(thr + 1)
    c_ge = count_ge(thr)
    need = (n_sel - c_gt).astype(F32)
    has_ties = jnp.max(c_ge) > n_sel

    def attend(kb, carry, nb, diag=False, ties=False):
        ms, eq_before = carry
        out = list(ms)
        start = lambda j: pl.multiple_of((kb + j) * tq, tq)
        sels = []
        for j in range(nb):
            key = keys_ref[pl.ds(start(j), tq), :]
            if ties:
                eq = jnp.where(key == thr, 1.0, 0.0)
                rank = _dot(tril_ref[...], eq.astype(BF16)) + eq_before
                sels.append((key > thr) | ((key == thr) & (rank <= need)))
                eq_before = eq_before + jnp.sum(eq, axis=0, keepdims=True)
            else:
                sels.append(key >= thr)

        def logits(u):
            h, j = u % N_HEADS, u // N_HEADS
            return _dot(k_ref[pl.ds(start(j), tq), h * LANES:(h + 1) * LANES], qt_ref[h * LANES:(h + 1) * LANES, :])

        def update(u, s):
            h, j = u % N_HEADS, u // N_HEADS
            vt = vt_ref[pl.ds(start(j), tq), h * LANES:(h + 1) * LANES]
            if diag:
                s, c = corr_ref[h] + s, 0.0
            else:
                c = _block_const(slope_ref, h, qb, kb + j, tq)
            s = jnp.where(sels[j], s, NEG)
            out[h] = _softmax_update(s, c, out[h], acc_ref, h, vt)

        _pipelined(nb * N_HEADS, logits, update)
        return tuple(out), eq_before

    def run(ties):
        acc_ref[...] = jnp.zeros_like(acc_ref)
        carry = (tuple(jnp.full((1, tq), NEG, F32) for _ in range(N_HEADS)), jnp.zeros((1, tq), F32))
        carry = _sweep(qb, lambda kb, c, nb: attend(kb, c, nb, ties=ties), carry, SWEEP_UNITS // N_HEADS)
        attend(qb, carry, 1, diag=True, ties=ties)

    @pl.when(has_ties)
    def _():
        run(True)

    @pl.when(jnp.logical_not(has_ties))
    def _():
        run(False)

    o_ref[...] = jnp.transpose(jnp.concatenate([_normalized(acc_ref, h) for h in range(N_HEADS)],
                                               axis=0)).astype(o_ref.dtype)


def _attn_c(main3, qaug, wt, k, ki, vt, corr, diagm, slopes, tril, n_sel):
    b, s_len, _ = k.shape
    nq = s_len // TQ
    return pl.pallas_call(
        functools.partial(_attn_c_kernel, n_sel=n_sel),
        grid=(b, nq),
        in_specs=[
            pl.BlockSpec((None, TQ, 256), lambda i, j: (i, j, 3)),
            pl.BlockSpec((None, TQ, 256), lambda i, j: (i, j, 6)),
            pl.BlockSpec(qaug.shape, lambda i, j: (0, 0, 0)),
            pl.BlockSpec((None, IDX_HEADS, TQ), lambda i, j: (i, 0, j)),
            pl.BlockSpec((None, s_len, N_HEADS * LANES), lambda i, j: (i, 0, 0)),
            pl.BlockSpec((None, s_len, IDX_DIM), lambda i, j: (i, 0, 0)),
            pl.BlockSpec((None, s_len, N_HEADS * LANES), lambda i, j: (i, 0, 0)),
            pl.BlockSpec((N_HEADS, TQ, TQ), lambda i, j: (0, 0, 0)),
            pl.BlockSpec((TQ, TQ), lambda i, j: (0, 0)),
            pl.BlockSpec(memory_space=pltpu.SMEM),
            pl.BlockSpec((TQ, TQ), lambda i, j: (0, 0)),
        ],
        out_specs=pl.BlockSpec((None, TQ, 256), lambda i, j: (i, j, 0)),
        out_shape=jax.ShapeDtypeStruct((b, s_len, 256), BF16),
        scratch_shapes=[pltpu.VMEM((s_len, TQ), I32),
                        pltpu.VMEM((s_len, TQ), I16),
                        pltpu.VMEM((s_len, TQ), I16),
                        pltpu.VMEM((N_HEADS, LANES, TQ), F32),
                        pltpu.VMEM((N_HEADS * LANES, TQ), BF16),
                        pltpu.VMEM((IDX_HEADS * IDX_DIM, TQ), BF16)],
        compiler_params=_cparams(("arbitrary", "arbitrary")),
        name="attn_sparse",
    )(main3, main3, qaug, wt, k, ki, vt, corr, diagm, slopes, tril)


def _band_bias_kernel(rb_ref, o_ref):
    h = pl.program_id(0)
    nk, tq = o_ref.shape
    j = lax.broadcasted_iota(I32, (nk, tq), 0)
    t = lax.broadcasted_iota(I32, (nk, tq), 1) + (nk - tq)
    idx = jnp.clip(t - j, -REL_CLIP, REL_CLIP) + REL_CLIP
    cq, ck = t // CHUNK, j // CHUNK
    valid = (ck <= cq) & (ck >= cq - BAND_CHUNKS)
    tbl = lax.fori_loop(0, 2 * REL_CLIP + 1,
                        lambda r, tb: jnp.where(idx == r, rb_ref[h, r], tb), jnp.zeros((nk, tq), F32))
    o_ref[...] = jnp.where(valid, tbl, NEG)


def _band_bias(rel_bias):
    nh = rel_bias.shape[0]
    return pl.pallas_call(
        _band_bias_kernel,
        grid=(nh,),
        in_specs=[pl.BlockSpec(memory_space=pltpu.SMEM)],
        out_specs=pl.BlockSpec((None, 3 * TQ, TQ), lambda h: (h, 0, 0)),
        out_shape=jax.ShapeDtypeStruct((nh, 3 * TQ, TQ), F32),
        compiler_params=_cparams(("arbitrary",)),
        name="band_bias",
    )(rel_bias)


def _attn_d_kernel(q_ref, k_ref, vt_ref, bias_ref, o_ref, qt_ref):
    qb = pl.program_id(1)
    tq = o_ref.shape[0]
    outs = [None] * N_HEADS
    _load_qt(q_ref, qt_ref, HEAD_DIM)

    def logits(h):
        ds = []
        for j in range(3):
            k0 = pl.multiple_of((qb + j) * tq, tq)
            ds.append(_dot(k_ref[pl.ds(k0, tq), h * LANES:(h + 1) * LANES], qt_ref[h * LANES:(h + 1) * LANES, :]))
        return ds

    def attend(h, ds):
        ss = []
        for j in range(3):
            s = bias_ref[h, j * tq:(j + 1) * tq, :] + ds[j]
            ss.append(jnp.where(qb + j >= 2, s, NEG))
        m = jnp.maximum(jnp.maximum(jnp.max(ss[0], axis=0, keepdims=True),
                                    jnp.max(ss[1], axis=0, keepdims=True)),
                        jnp.max(ss[2], axis=0, keepdims=True))
        acc = jnp.zeros((LANES, tq), F32)
        for j in range(3):
            k0 = pl.multiple_of((qb + j) * tq, tq)
            vt = vt_ref[pl.ds(k0, tq), h * LANES:(h + 1) * LANES]
            acc = acc + _dot_t(vt, jnp.exp(ss[j] - m).astype(BF16))
        outs[h] = acc[:HEAD_DIM] / acc[HEAD_DIM:HEAD_DIM + 1]

    _pipelined(N_HEADS, logits, attend, lookahead=1)
    o_ref[...] = jnp.transpose(jnp.concatenate(outs, axis=0)).astype(o_ref.dtype)


def _attn_d(main3, k_pad, vt_pad, bias):
    b, sp, _ = k_pad.shape
    s_len = sp - 2 * TQ
    nq = s_len // TQ
    return pl.pallas_call(
        _attn_d_kernel,
        grid=(b, nq),
        in_specs=[
            pl.BlockSpec((None, TQ, 256), lambda i, j: (i, j, 7)),
            pl.BlockSpec((None, sp, N_HEADS * LANES), lambda i, j: (i, 0, 0)),
            pl.BlockSpec((None, sp, N_HEADS * LANES), lambda i, j: (i, 0, 0)),
            pl.BlockSpec((N_HEADS, 3 * TQ, TQ), lambda i, j: (0, 0, 0)),
        ],
        out_specs=pl.BlockSpec((None, TQ, 256), lambda i, j: (i, j, 0)),
        out_shape=jax.ShapeDtypeStruct((b, s_len, 256), BF16),
        scratch_shapes=[pltpu.VMEM((N_HEADS * LANES, TQ), BF16)],
        compiler_params=_cparams(("arbitrary", "arbitrary")),
        name="attn_band",
    )(main3, k_pad, vt_pad, bias)


def _outproj_kernel(x_ref, oa_ref, ob_ref, oc_ref, od_ref, w_ref, gate_ref, o_ref):
    y = _dot(oa_ref[...], w_ref[0:256, :])
    y = y + _dot(ob_ref[...], w_ref[256:512, :])
    y = y + _dot(oc_ref[...], w_ref[512:768, :])
    y = y + _dot(od_ref[...], w_ref[768:1024, :])
    o_ref[...] = x_ref[...] + gate_ref[...] * y


def _outproj(x2, oa, ob, oc, od, w, mod3, s_len):
    n, d = x2.shape
    tm = TM_PROJ
    tpb = s_len // tm
    ospec = pl.BlockSpec((tm, 256), lambda i: (i, 0))
    return pl.pallas_call(
        _outproj_kernel,
        grid=(n // tm,),
        in_specs=[pl.BlockSpec((tm, d), lambda i: (i, 0)), ospec, ospec, ospec, ospec,
                  pl.BlockSpec((d, d), lambda i: (0, 0)),
                  pl.BlockSpec((None, 1, d), lambda i: (i // tpb, 0, 2))],
        out_specs=pl.BlockSpec((tm, d), lambda i: (i, 0)),
        out_shape=jax.ShapeDtypeStruct((n, d), F32),
        compiler_params=_cparams(("arbitrary",)),
        name="outproj",
    )(x2, oa, ob, oc, od, w, mod3)


def _ffn_kernel(x_ref, g_ref, sh_ref, sc_ref, gate_ref, w1_ref, w3_ref, w2_ref, o_ref,
                hn_ref, acc_ref):
    f = pl.program_id(1)

    @pl.when(f == 0)
    def _():
        hn_ref[...] = _adaln(x_ref[...], g_ref[...], sh_ref[...], sc_ref[...]).astype(BF16)
        acc_ref[...] = jnp.zeros_like(acc_ref)

    hn = hn_ref[...]
    a = _silu(_dot(hn, w1_ref[...])) * _dot(hn, w3_ref[...])
    acc_ref[...] += _dot(a.astype(BF16), w2_ref[...])

    @pl.when(f == pl.num_programs(1) - 1)
    def _():
        o_ref[...] = x_ref[...] + gate_ref[...] * acc_ref[...]


def _ffn(x2, g, mod3, w1, w3, w2, s_len):
    n, d = x2.shape
    dff = w1.shape[1]
    tm = TM_FFN
    tf = dff // 2
    tpb = s_len // tm
    return pl.pallas_call(
        _ffn_kernel,
        grid=(n // tm, dff // tf),
        in_specs=[
            pl.BlockSpec((tm, d), lambda i, f: (i, 0)),
            pl.BlockSpec((1, d), lambda i, f: (0, 0)),
            pl.BlockSpec((None, 1, d), lambda i, f: (i // tpb, 0, 3)),
            pl.BlockSpec((None, 1, d), lambda i, f: (i // tpb, 0, 4)),
            pl.BlockSpec((None, 1, d), lambda i, f: (i // tpb, 0, 5)),
            pl.BlockSpec((d, tf), lambda i, f: (0, f)),
            pl.BlockSpec((d, tf), lambda i, f: (0, f)),
            pl.BlockSpec((tf, d), lambda i, f: (f, 0)),
        ],
        out_specs=pl.BlockSpec((tm, d), lambda i, f: (i, 0)),
        out_shape=jax.ShapeDtypeStruct((n, d), F32),
        scratch_shapes=[pltpu.VMEM((tm, d), BF16), pltpu.VMEM((tm, d), F32)],
        compiler_params=_cparams(("arbitrary", "arbitrary")),
        name="ffn_dense",
    )(x2, g, mod3, mod3, mod3, w1, w3, w2)


def _router_kernel(x_ref, g_ref, sh_ref, sc_ref, rhi_ref, rlo_ref, tril_ref,
                   hn_ref, comb_ref, pos_ref, post_ref, cum_ref):
    sub = tril_ref.shape[0]
    carry = jnp.zeros((1, LANES), F32)
    cum_ref[...] = jnp.zeros_like(cum_ref)
    for r in range(x_ref.shape[0] // sub):
        rows = slice(r * sub, (r + 1) * sub)
        hn = _adaln(x_ref[rows, :], g_ref[...], sh_ref[...], sc_ref[...])
        hi = hn.astype(BF16)
        hn_ref[rows, :] = hi
        lo = (hn - hi.astype(F32)).astype(BF16)
        logits = _dot(hi, rhi_ref[...]) + (_dot(hi, rlo_ref[...]) + _dot(lo, rhi_ref[...]))
        lane = lax.broadcasted_iota(I32, logits.shape, 1)
        logits = jnp.where(lane < N_EXPERTS, logits, NEG)
        m1 = jnp.max(logits, axis=-1, keepdims=True)
        i1 = jnp.min(jnp.where(logits == m1, lane, LANES), axis=-1, keepdims=True)
        rest = jnp.where(lane == i1, NEG, logits)
        m2 = jnp.max(rest, axis=-1, keepdims=True)
        i2 = jnp.min(jnp.where(rest == m2, lane, LANES), axis=-1, keepdims=True)
        e2 = jnp.exp(m2 - m1)
        g1 = 1.0 / (1.0 + e2)
        g2 = e2 / (1.0 + e2)
        comb_ref[rows, :] = jnp.where(lane == i1, g1, 0.0) + jnp.where(lane == i2, g2, 0.0)
        routed = jnp.where((lane == i1) | (lane == i2), 1.0, 0.0)
        incl = _dot(tril_ref[...], routed.astype(BF16)) + carry
        pos = jnp.where(routed > 0.0, incl - 1.0, -1.0)
        pos_ref[rows, :] = pos
        post_ref[:, rows] = jnp.transpose(pos)[:N_EXPERTS, :]
        carry = carry + jnp.sum(routed, axis=0, keepdims=True)
        cum_ref[r + 1:r + 2, :] = carry


def _router(x2, g, mod3, rhi, rlo, tril, s_len):
    n, d = x2.shape
    tm = min(TM_MOE, s_len)
    tpb = s_len // tm
    ns = n // tm
    return pl.pallas_call(
        _router_kernel,
        grid=(ns,),
        in_specs=[
            pl.BlockSpec((tm, d), lambda i: (i, 0)),
            pl.BlockSpec((1, d), lambda i: (0, 0)),
            pl.BlockSpec((None, 1, d), lambda i: (i // tpb, 0, 3)),
            pl.BlockSpec((None, 1, d), lambda i: (i // tpb, 0, 4)),
            pl.BlockSpec((d, LANES), lambda i: (0, 0)),
            pl.BlockSpec((d, LANES), lambda i: (0, 0)),
            pl.BlockSpec(tril.shape, lambda i: (0, 0)),
        ],
        out_specs=[
            pl.BlockSpec((tm, d), lambda i: (i, 0)),
            pl.BlockSpec((tm, LANES), lambda i: (i, 0)),
            pl.BlockSpec((tm, LANES), lambda i: (i, 0)),
            pl.BlockSpec((None, N_EXPERTS, tm), lambda i: (i, 0, 0)),
            pl.BlockSpec((None, 8, LANES), lambda i: (i, 0, 0)),
        ],
        out_shape=[
            jax.ShapeDtypeStruct((n, d), BF16),
            jax.ShapeDtypeStruct((n, LANES), F32),
            jax.ShapeDtypeStruct((n, LANES), F32),
            jax.ShapeDtypeStruct((ns, N_EXPERTS, tm), F32),
            jax.ShapeDtypeStruct((ns, 8, LANES), F32),
        ],
        compiler_params=_cparams(("arbitrary",)),
        name="router",
    )(x2, g, mod3, mod3, rhi, rlo, tril)


def _moe_kernel(cum_ref, hn_ref, comb_ref, pos_ref, post_ref, w1_ref, w3_ref, w2_ref, o_ref,
                xc_ref, y_ref):
    s, e, f = pl.program_id(0), pl.program_id(1), pl.program_id(2)
    nf = pl.num_programs(2)
    tm = hn_ref.shape[0]
    ch, sub = MOE_CHUNK, MOE_SUB
    nsub = tm // sub
    cum = [cum_ref[(s * (nsub + 1) + r) * N_EXPERTS + e] for r in range(nsub + 1)]
    nch = (cum[nsub] + (ch - 1)) // ch

    @pl.when((e == 0) & (f == 0))
    def _():
        o_ref[...] = jnp.zeros_like(o_ref)

    @pl.when(f == 0)
    def _():
        def clear(c, carry):
            r0 = pl.multiple_of(c * ch, ch)
            xc_ref[pl.ds(r0, ch), :] = jnp.zeros((ch, xc_ref.shape[1]), BF16)
            y_ref[pl.ds(r0, ch), :] = jnp.zeros((ch, y_ref.shape[1]), F32)
            return carry
        lax.fori_loop(0, nch + 1, clear, 0)

        slot = lax.broadcasted_iota(I32, (ch, sub), 0).astype(F32)
        for r in range(nsub):
            prow = post_ref[pl.ds(e, 1), r * sub:(r + 1) * sub]

            def gather(c, carry, r=r, prow=prow):
                r0 = pl.multiple_of(c * ch, ch)
                sel = jnp.where(prow - (c * ch).astype(F32) == slot, 1.0, 0.0).astype(BF16)
                rows = _dot(sel, hn_ref[r * sub:(r + 1) * sub, :])
                xc_ref[pl.ds(r0, ch), :] = (xc_ref[pl.ds(r0, ch), :].astype(F32) + rows).astype(BF16)
                return carry
            lax.fori_loop(cum[r] // ch, (cum[r + 1] + (ch - 1)) // ch, gather, 0)

    def expert(c, carry):
        r0 = pl.multiple_of(c * ch, ch)
        xc = xc_ref[pl.ds(r0, ch), :]
        a = _silu(_dot(xc, w1_ref[...])) * _dot(xc, w3_ref[...])
        y_ref[pl.ds(r0, ch), :] += _dot(a.astype(BF16), w2_ref[...])
        return carry
    lax.fori_loop(0, nch, expert, 0)

    @pl.when(f == nf - 1)
    def _():
        lane = lax.broadcasted_iota(I32, (tm, LANES), 1)
        pcol = jnp.sum(jnp.where(lane == e, pos_ref[...], 0.0), axis=-1, keepdims=True)
        gcol = jnp.sum(jnp.where(lane == e, comb_ref[...], 0.0), axis=-1, keepdims=True)
        slot = lax.broadcasted_iota(I32, (sub, 2 * ch), 1).astype(F32)
        for r in range(nsub):
            rows = slice(r * sub, (r + 1) * sub)

            def scatter(c2, carry, rows=rows):
                r0 = pl.multiple_of(c2 * (2 * ch), 2 * ch)
                z = y_ref[pl.ds(r0, 2 * ch), :].astype(BF16)
                sel = jnp.where(pcol[rows] - (c2 * (2 * ch)).astype(F32) == slot, 1.0, 0.0).astype(BF16)
                o_ref[rows, :] += gcol[rows] * _dot(sel, z)
                return carry
            lax.fori_loop(cum[r] // (2 * ch), (cum[r + 1] + (2 * ch - 1)) // (2 * ch), scatter, 0)


def _moe(hn, comb, pos, post, cum, w1, w3, w2):
    n, d = hn.shape
    ne, _, dff = w1.shape
    ns, _, tm = post.shape
    tf = dff // 4
    cap = tm + 2 * MOE_CHUNK
    grid_spec = pltpu.PrefetchScalarGridSpec(
        num_scalar_prefetch=1,
        grid=(ns, ne, dff // tf),
        in_specs=[
            pl.BlockSpec((tm, d), lambda i, e, f, c: (i, 0)),
            pl.BlockSpec((tm, LANES), lambda i, e, f, c: (i, 0)),
            pl.BlockSpec((tm, LANES), lambda i, e, f, c: (i, 0)),
            pl.BlockSpec((None, N_EXPERTS, tm), lambda i, e, f, c: (i, 0, 0)),
            pl.BlockSpec((None, d, tf), lambda i, e, f, c: (e, 0, f)),
            pl.BlockSpec((None, d, tf), lambda i, e, f, c: (e, 0, f)),
            pl.BlockSpec((None, tf, d), lambda i, e, f, c: (e, f, 0)),
        ],
        out_specs=pl.BlockSpec((tm, d), lambda i, e, f, c: (i, 0)),
        scratch_shapes=[pltpu.VMEM((cap, d), BF16), pltpu.VMEM((cap, d), F32)],
    )
    return pl.pallas_call(
        _moe_kernel,
        grid_spec=grid_spec,
        out_shape=jax.ShapeDtypeStruct((n, d), F32),
        compiler_params=_cparams(("arbitrary", "arbitrary", "arbitrary")),
        name="ffn_experts",
    )(cum, hn, comb, pos, post, w1, w3, w2)


def _residual_kernel(x_ref, f_ref, gate_ref, gn_ref, o_ref, *, final):
    y = x_ref[...] + gate_ref[...] * f_ref[...]
    o_ref[...] = _rms(y, gn_ref[...]) if final else y


def _residual(x2, f2, mod3, gn, s_len, final):
    n, d = x2.shape
    tm = TM_FFN
    tpb = s_len // tm
    row = pl.BlockSpec((tm, d), lambda i: (i, 0))
    return pl.pallas_call(
        functools.partial(_residual_kernel, final=final),
        grid=(n // tm,),
        in_specs=[row, row, pl.BlockSpec((None, 1, d), lambda i: (i // tpb, 0, 5)),
                  pl.BlockSpec((1, d), lambda i: (0, 0))],
        out_specs=row,
        out_shape=jax.ShapeDtypeStruct((n, d), F32),
        compiler_params=_cparams(("arbitrary",)),
        name="residual",
    )(x2, f2, mod3, gn)


def _final_kernel(x_ref, g_ref, o_ref):
    o_ref[...] = _rms(x_ref[...], g_ref[...])


def _final_norm(x2, g):
    n, d = x2.shape
    tm = TM_FFN
    return pl.pallas_call(
        _final_kernel,
        grid=(n // tm,),
        in_specs=[pl.BlockSpec((tm, d), lambda i: (i, 0)), pl.BlockSpec((1, d), lambda i: (0, 0))],
        out_specs=pl.BlockSpec((tm, d), lambda i: (i, 0)),
        out_shape=jax.ShapeDtypeStruct((n, d), F32),
        compiler_params=_cparams(("arbitrary",)),
        name="final_norm",
    )(x2, g)


def _alibi_slopes():
    return 2.0 ** (-8.0 * jnp.arange(1, N_HEADS + 1, dtype=F32) / N_HEADS)


def _block_tables():
    j = jnp.arange(TQ)[:, None]
    i = jnp.arange(TQ)[None, :]
    diagm = jnp.where((j // CHUNK) <= (i // CHUNK), 0.0, NEG).astype(F32)
    corr = -2.0 * _alibi_slopes()[:, None, None] * jnp.maximum(j - i, 0).astype(F32)[None]
    return corr, diagm


def _rope_tables(s_len):
    inv = ROPE_THETA ** (-jnp.arange(0, QK_ROPE, 2, dtype=F32) / QK_ROPE)
    ang = jnp.arange(s_len, dtype=F32)[:, None] * inv[None, :]
    cos2 = jnp.concatenate([jnp.cos(ang)] * 2, axis=-1)
    sin2 = jnp.concatenate([jnp.sin(ang)] * 2, axis=-1)
    z = jnp.zeros((s_len, QK_NOPE), F32)
    z2 = jnp.zeros((s_len, LANES - QK_NOPE - QK_ROPE), F32)
    return (jnp.concatenate([z, cos2, z2], axis=-1), jnp.concatenate([z, sin2, z2], axis=-1))


def _rot_cols(w):
    half = QK_ROPE // 2
    return jnp.concatenate([-w[..., half:], w[..., :half]], axis=-1)


def _slabs(a, nh, extra=None):
    b, s_len, _ = a.shape
    parts = [a.reshape(b, s_len, nh, -1)]
    if extra is not None:
        parts.append(jnp.broadcast_to(extra, (b, s_len, nh, extra.shape[-1])).astype(a.dtype))
    used = sum(p.shape[-1] for p in parts)
    parts.append(jnp.zeros((b, s_len, nh, LANES - used), a.dtype))
    return jnp.concatenate(parts, axis=-1)


def _rows(slab):
    b, s_len = slab.shape[:2]
    return slab.reshape(b, s_len, -1)


def _alibi_extras(s_len, nslab):
    loc = (jnp.arange(s_len) % TQ).astype(F32)[:, None, None]
    k_extra = jnp.concatenate([jnp.ones((s_len, nslab, 1), F32),
                               jnp.broadcast_to(loc, (s_len, nslab, 1))], axis=-1)
    sl = _alibi_slopes()[:, None, None]
    t_loc = jnp.arange(TQ, dtype=F32)[None, None, :]
    q_rows = jnp.concatenate([-sl * t_loc, jnp.broadcast_to(sl, (N_HEADS, 1, TQ)),
                              jnp.zeros((N_HEADS, 6, TQ), F32)], axis=1)
    return q_rows, k_extra[None]


def _layer_weights(l, w_in, mla_w_uq, mla_w_ukv):
    w = w_in[l]
    names = ('a_q', 'a_k', 'a_v', 'b_qd', 'b_kvd', 'b_kr', 'c_q', 'c_k', 'c_v', 'c_qi', 'c_ki',
             'c_wi', 'd_q', 'd_k', 'd_v')
    widths = (256, 256, 256, Q_LORA, KV_LORA, QK_ROPE, 256, 256, 256, IDX_HEADS * IDX_DIM, IDX_DIM,
              IDX_HEADS, 256, 256, 256)
    cols, o = {}, 0
    for nme, wd in zip(names, widths):
        cols[nme] = w[:, o:o + wd]
        o += wd
    pad = jnp.zeros((w.shape[0], N_SIDE - (Q_LORA + KV_LORA + 2 * QK_ROPE + IDX_DIM + IDX_HEADS)), F32)
    wcat = jnp.concatenate(
        [cols['a_q'], cols['a_k'], cols['a_v'], cols['c_q'], cols['c_k'], cols['c_v'], cols['c_qi'],
         cols['d_q'], cols['d_k'], cols['d_v'],
         cols['b_qd'], cols['b_kvd'], cols['b_kr'], _rot_cols(cols['b_kr']), cols['c_ki'],
         cols['c_wi'], pad], axis=-1).astype(BF16)
    ones = lambda k: jnp.ones((k,), F32)
    cs = jnp.concatenate([
        ones(256) * DIFF_D ** -0.5, ones(512),
        ones(256) * HEAD_DIM ** -0.5, ones(768),
        ones(256) * HEAD_DIM ** -0.5, ones(512), ones(N_SIDE)])[None, :]

    uq = mla_w_uq[l].reshape(Q_LORA, N_HEADS, QK_NOPE + QK_ROPE)
    zq = jnp.zeros((Q_LORA, N_HEADS, LANES - QK_NOPE - QK_ROPE), F32)
    wqa = jnp.concatenate([uq, zq], axis=-1).reshape(Q_LORA, N_HEADS * LANES).astype(BF16)
    wqb = jnp.concatenate([jnp.zeros((Q_LORA, N_HEADS, QK_NOPE), F32), _rot_cols(uq[..., QK_NOPE:]), zq],
                          axis=-1).reshape(Q_LORA, N_HEADS * LANES).astype(BF16)
    ukv = mla_w_ukv[l].reshape(KV_LORA, N_HEADS, QK_NOPE + HEAD_DIM)
    zk = jnp.zeros((KV_LORA, N_HEADS, LANES - QK_NOPE), F32)
    wk = jnp.concatenate([ukv[..., :QK_NOPE], zk], axis=-1).reshape(KV_LORA, N_HEADS * LANES).astype(BF16)
    wv = jnp.concatenate([ukv[..., QK_NOPE:], zk], axis=-1).reshape(KV_LORA, N_HEADS * LANES).astype(BF16)
    return wcat, cs, wqa, wqb, wk, wv


def _placement():
    r = jnp.arange(LANES)[:, None]
    c = jnp.arange(N_HEADS * LANES)[None, :] % LANES
    e1 = ((c >= QK_NOPE) & (c < QK_NOPE + QK_ROPE) & (r == c - QK_NOPE)).astype(BF16)
    e2 = ((c >= QK_NOPE) & (c < QK_NOPE + QK_ROPE) & (r == c - QK_NOPE + QK_ROPE)).astype(BF16)
    return e1, e2


def kernel(x, c, ada_w, ada_b, mix_norm_g, ffn_norm_g, w_in, w_out, diff_lambda, diff_norm_g, mla_q_norm_g, mla_kv_norm_g, mla_w_uq, mla_w_ukv, band_rel_bias, ffn_w1, ffn_w3, ffn_w2, moe_router, moe_w1, moe_w3, moe_w2, final_norm_g):
    b, s_len, d = x.shape
    depth = ada_w.shape[0]
    n = b * s_len
    n_sel = min(TOPK_MAX, s_len // 4)
    assert d == D_MODEL and s_len % TM_FFN == 0 and s_len % TQ == 0

    mod = _modulation(c, ada_w, ada_b)
    corr, diagm = _block_tables()
    slopes = _alibi_slopes()
    cos_t, sin_t = _rope_tables(s_len)
    e1, e2 = _placement()
    tril = (jnp.arange(TQ)[None, :] <= jnp.arange(TQ)[:, None]).astype(BF16)
    mla_scale = (QK_NOPE + QK_ROPE) ** -0.5
    qaug, kx8 = _alibi_extras(s_len, 2 * N_HEADS)
    _, kx4 = _alibi_extras(s_len, N_HEADS)
    one4 = jnp.ones((1, s_len, N_HEADS, 1), F32)
    tr = lambda a: jnp.swapaxes(a, 1, 2)
    front = 2 * TQ

    x2 = x.reshape(n, d)
    for l in range(depth):
        mod3 = mod[l].reshape(b, 1, 6 * d)
        wcat, cs, wqa, wqb, wk, wv = _layer_weights(l, w_in, mla_w_uq, mla_w_ukv)
        main, side = _inproj(x2, mix_norm_g[l][None], mod3, wcat, cs, s_len)
        main3 = main.reshape(b, s_len, N_MAIN)
        side3 = side.reshape(b, s_len, N_SIDE)
        blk = lambda k: main3[:, :, k * 256:(k + 1) * 256]

        lam_init = 0.8 - 0.6 * math.exp(-0.3 * l)
        o_a = _attn_a(main3, qaug, _rows(_slabs(blk(1), 2 * N_HEADS, kx8)),
                      _rows(_slabs(blk(2), N_HEADS, one4)), corr + diagm[None], slopes, diff_lambda[l],
                      jnp.broadcast_to(diff_norm_g[l][:, None], (2 * DIFF_D, TQ)), lam_init)

        qb_, kb_, vb_ = _mla_prep(side, mla_q_norm_g[l][None], mla_kv_norm_g[l][None], wqa, wqb, wk, wv,
                                  e1, e2, cos_t, sin_t, s_len, mla_scale)
        o_b = _attn_b(qb_.reshape(b, s_len, -1), kb_.reshape(b, s_len, -1),
                      vb_.reshape(b, s_len, -1), diagm)

        o0 = Q_LORA + KV_LORA + 2 * QK_ROPE
        ki = side3[:, :, o0:o0 + IDX_DIM].astype(BF16)
        wt = tr(side3[:, :, o0 + IDX_DIM:o0 + IDX_DIM + IDX_HEADS])
        o_c = _attn_c(main3, qaug, wt, _rows(_slabs(blk(4), N_HEADS, kx4)),
                      ki, _rows(_slabs(blk(5), N_HEADS, one4)), corr, diagm, slopes, tril, n_sel)

        k_pad = jnp.pad(_rows(_slabs(blk(8), N_HEADS)), ((0, 0), (front, 0), (0, 0)))
        vt_pad = jnp.pad(_rows(_slabs(blk(9), N_HEADS, one4)), ((0, 0), (front, 0), (0, 0)))
        o_d = _attn_d(main3, k_pad, vt_pad, _band_bias(band_rel_bias[l]))

        flat = lambda a: a.reshape(n, 256)
        x2 = _outproj(x2, flat(o_a), flat(o_b), flat(o_c), flat(o_d), w_out[l].astype(BF16), mod3, s_len)

        gf = ffn_norm_g[l][None]
        if l % 2 == 0:
            i = l // 2
            x2 = _ffn(x2, gf, mod3, ffn_w1[i].astype(BF16), ffn_w3[i].astype(BF16),
                      ffn_w2[i].astype(BF16), s_len)
        else:
            i = l // 2
            r = jnp.pad(moe_router[i], ((0, 0), (0, LANES - N_EXPERTS)))
            rhi = r.astype(BF16)
            rlo = (r - rhi.astype(F32)).astype(BF16)
            tril_sub = (jnp.arange(MOE_SUB)[None, :] <= jnp.arange(MOE_SUB)[:, None]).astype(BF16)
            hn, comb, pos, post, cum = _router(x2, gf, mod3, rhi, rlo, tril_sub, s_len)
            nsub = post.shape[-1] // MOE_SUB
            cum_i = cum[:, :nsub + 1, :N_EXPERTS].astype(I32).reshape(-1)
            f2 = _moe(hn, comb, pos, post, cum_i, moe_w1[i].astype(BF16), moe_w3[i].astype(BF16),
                      moe_w2[i].astype(BF16))
            last = l == depth - 1
            x2 = _residual(x2, f2, mod3, final_norm_g[None], s_len, last)
            if last:
                return x2.reshape(b, s_len, d)
    return _final_norm(x2, final_norm_g[None]).reshape(b, s_len, d)
```
